```python
import jax, jax.numpy as jnp
from jax import lax
import numpy as np

D_MODEL = 1024
BATCH = 8
SEQ = 2048
DEPTH = 4
DEC_BATCH = 128
DEC_SEQ = 4
PAST_LEN = 8192
PAGE_SIZE = 128

HEAD_DIM = 64
DIL_PAIRS = ((128, 1), (512, 4), (2048, 16))
N_DIL = len(DIL_PAIRS)
H_A = D_MODEL // 2 // HEAD_DIM
H_B = D_MODEL // 2 // HEAD_DIM
H_BKV = 2
G_B = H_B // H_BKV
SWA_WINDOW = 128
H_R = 4
DK_R = D_MODEL // H_R
DV_R = 2 * DK_R
RET_CHUNK = 128
D_FF = 4 * D_MODEL
ROPE_THETA = 10000.0
BLOCK = 128
NORM_EPS = 1e-6
GN_EPS = 1e-5
NEG_INF = -1e30
N_EVEN = (DEPTH + 1) // 2
N_ODD = DEPTH // 2
A_COLS = N_DIL * 3 * H_A * HEAD_DIM
B_Q = H_B * HEAD_DIM
B_KV = H_BKV * HEAD_DIM
MIX_IN = A_COLS + B_Q + 2 * B_KV
MIX_OUT = (H_A + H_B) * HEAD_DIM
RET_QK = H_R * DK_R
RET_V = H_R * DV_R
RET_IN = 2 * RET_QK + 2 * RET_V

kernel_name = 'hybrid_dilated_swa_retention_decoder_step'


def _rmsnorm(x, g):
    xf = x.astype(jnp.float32)
    y = xf * lax.rsqrt(jnp.mean(xf * xf, axis=-1, keepdims=True) + NORM_EPS)
    return (y * g.astype(jnp.float32)).astype(x.dtype)


def _rope(x, pos):
    dh = x.shape[-1]
    half = dh // 2
    inv = ROPE_THETA ** (-jnp.arange(half, dtype=jnp.float32) * (2.0 / dh))
    ang = jnp.asarray(pos, jnp.float32)[:, None] * inv[None, :]
    ang = ang.reshape((x.shape[1],) + (1,) * (x.ndim - 3) + (half,))
    c, s = jnp.cos(ang), jnp.sin(ang)
    xf = x.astype(jnp.float32)
    x1, x2 = xf[..., :half], xf[..., half:]
    return jnp.concatenate([x1 * c - x2 * s, x2 * c + x1 * s], axis=-1).astype(x.dtype)


def _attend(q, k, v, mask, sink=None):
    qf = q.astype(jnp.float32) * (q.shape[-1] ** -0.5)
    s = jnp.einsum('...qhgd,...khd->...hgqk', qf, k.astype(jnp.float32))
    s = jnp.where(mask, s, NEG_INF)
    m = jnp.max(s, axis=-1)
    if sink is not None:
        sk = sink.astype(jnp.float32)[:, :, None]
        m = jnp.maximum(m, sk)
    p = jnp.exp(s - m[..., None])
    l = jnp.sum(p, axis=-1)
    if sink is not None:
        l = l + jnp.exp(sk - m)
    o = jnp.einsum('...hgqk,...khd->...qhgd', p, v.astype(jnp.float32))
    l_q = jnp.moveaxis(l, -1, -3)
    o = o / l_q[..., None]
    lse = jnp.moveaxis(m, -1, -3) + jnp.log(l_q)
    return o.astype(q.dtype), lse


def _banded(q, k, v, window_back, sink=None):
    N, L, Hk, G, Dh = q.shape
    nb = -(-L // BLOCK)
    Lp = nb * BLOCK
    qb = jnp.pad(q, ((0, 0), (0, Lp - L), (0, 0), (0, 0), (0, 0))).reshape(N, nb, BLOCK, Hk, G, Dh)

    def blocks(x):
        xp = jnp.pad(x, ((0, 0), (BLOCK, Lp - L), (0, 0), (0, 0)))
        return jnp.concatenate([xp[:, :Lp].reshape(N, nb, BLOCK, Hk, Dh),
                                xp[:, BLOCK:].reshape(N, nb, BLOCK, Hk, Dh)], axis=2)

    kb, vb = blocks(k), blocks(v)
    blk = np.arange(nb)[:, None, None]
    i = np.arange(BLOCK)[None, :, None]
    j = np.arange(2 * BLOCK)[None, None, :]
    diff = i + BLOCK - j
    mask = (diff >= 0) & (diff <= window_back) & (blk * BLOCK - BLOCK + j >= 0)
    mask = mask[None, :, None, None]
    o, lse = _attend(qb, kb, vb, mask, sink)
    return o.reshape(N, Lp, Hk, G, Dh)[:, :L], lse.reshape(N, Lp, Hk, G)[:, :L]


def _dilated_prompt(q, k, v, dil, span):
    N, S, H, Dh = q.shape
    Ls = S // dil

    def to_res(x):
        return x.reshape(N, Ls, dil, H, Dh).swapaxes(1, 2).reshape(N * dil, Ls, H, Dh)

    o, lse = _banded(to_res(q)[:, :, :, None], to_res(k), to_res(v), span)
    o = o[:, :, :, 0].reshape(N, dil, Ls, H, Dh).swapaxes(1, 2).reshape(N, S, H, Dh)
    lse = lse[:, :, :, 0].reshape(N, dil, Ls, H).swapaxes(1, 2).reshape(N, S, H)
    return o, lse


def _ab_project(h, w_in, pos):
    N, L, _ = h.shape
    z = h @ w_in
    za = z[..., :A_COLS].reshape(N, L, N_DIL, 3, H_A, HEAD_DIM)
    qa = _rope(za[:, :, :, 0], pos)
    ka = _rope(za[:, :, :, 1], pos)
    va = za[:, :, :, 2]
    zb = z[..., A_COLS:]
    qb = _rope(zb[..., :B_Q].reshape(N, L, H_BKV, G_B, HEAD_DIM), pos)
    kb = _rope(zb[..., B_Q:B_Q + B_KV].reshape(N, L, H_BKV, HEAD_DIM), pos)
    vb = zb[..., B_Q + B_KV:].reshape(N, L, H_BKV, HEAD_DIM)
    return qa, ka, va, qb, kb, vb


def _ab_merge(oa_list, lse_list, ob, w_out):
    lse = jnp.stack(lse_list, axis=2)
    wts = jax.nn.softmax(lse, axis=2)
    oa = jnp.sum(wts[..., None] * jnp.stack(oa_list, axis=2).astype(jnp.float32), axis=2).astype(ob.dtype)
    N, L = oa.shape[:2]
    y = jnp.concatenate([oa.reshape(N, L, H_A * HEAD_DIM), ob.reshape(N, L, H_B * HEAD_DIM)], axis=-1)
    return y @ w_out


def _ab_prompt(h, pos, w_in, w_out, sinks):
    S = h.shape[1]
    qa, ka, va, qb, kb, vb = _ab_project(h, w_in, pos)
    oa_list, lse_list, rows = [], [], []
    for g, (win, dil) in enumerate(DIL_PAIRS):
        o, lse = _dilated_prompt(qa[:, :, g], ka[:, :, g], va[:, :, g], dil, win // dil)
        oa_list.append(o)
        lse_list.append(lse)
        wb = min(win, S)
        rows.append(jnp.stack([ka[:, S - wb:, g], va[:, S - wb:, g]], axis=2))
    ob, _ = _banded(qb, kb, vb, SWA_WINDOW - 1, sinks.reshape(H_BKV, G_B))
    wb = min(SWA_WINDOW, S)
    swa_rows = jnp.stack([kb[:, S - wb:], vb[:, S - wb:]], axis=2)
    return _ab_merge(oa_list, lse_list, ob, w_out), rows, swa_rows


def _ab_sample(h, pos, dil_caches, cache_swa, w_in, w_out, sinks):
    T = h.shape[1]
    qa, ka, va, qb, kb, vb = _ab_project(h, w_in, pos)
    oa_list, lse_list, rows = [], [], []
    for g, (win, dil) in enumerate(DIL_PAIRS):
        cache = dil_caches[g]
        wb = cache.shape[1]
        m = np.arange(win // dil + 1)
        idx = wb + np.arange(T)[:, None] - dil * m[None, :]
        from_new = idx >= wb
        valid = idx >= 0
        kv_new = jnp.stack([ka[:, :, g], va[:, :, g]], axis=2)
        g_cache = cache[:, np.clip(idx, 0, wb - 1)]
        g_new = kv_new[:, np.clip(idx - wb, 0, T - 1)]
        kv = jnp.where(from_new[None, :, :, None, None, None], g_new, g_cache)
        mask = valid.reshape(1, T, 1, 1, 1, -1)
        o, lse = _attend(qa[:, :, g][:, :, None, :, None, :], kv[:, :, :, 0], kv[:, :, :, 1], mask)
        oa_list.append(o[:, :, 0, :, 0])
        lse_list.append(lse[:, :, 0, :, 0])
        rows.append(kv_new)
    wb = cache_swa.shape[1]
    kc = jnp.concatenate([cache_swa[:, :, 0], kb], axis=1)
    vc = jnp.concatenate([cache_swa[:, :, 1], vb], axis=1)
    kpos = PAST_LEN - wb + np.arange(wb + T)
    diff = np.asarray(pos)[:, None] - kpos[None, :]
    mask = ((diff >= 0) & (diff <= SWA_WINDOW - 1))[None, None, None]
    ob, _ = _attend(qb, kc, vc, mask, sinks.reshape(H_BKV, G_B))
    swa_rows = jnp.stack([kb, vb], axis=2)
    return _ab_merge(oa_list, lse_list, ob, w_out), rows, swa_rows


def _retention(q, k, v, state):
    N, L, H, Dk = q.shape
    Dv = v.shape[-1]
    C = RET_CHUNK if L % RET_CHUNK == 0 else L
    nc = L // C
    log_g = jnp.log(1.0 - 2.0 ** (-5.0 - jnp.arange(H, dtype=jnp.float32)))
    ii = jnp.arange(C, dtype=jnp.float32)
    diff = ii[:, None] - ii[None, :]
    decay = jnp.where(diff >= 0, jnp.exp(jnp.maximum(diff, 0.0)[None] * log_g[:, None, None]), 0.0)
    q_dec = jnp.exp((ii + 1.0)[:, None] * log_g[None, :])
    k_dec = jnp.exp((C - 1.0 - ii)[:, None] * log_g[None, :])
    c_dec = jnp.exp(C * log_g)

    def to_chunks(x):
        return x.astype(jnp.float32).reshape(N, nc, C, H, x.shape[-1]).swapaxes(0, 1)

    def step(S, xs):
        qc, kc, vc = xs
        a = jnp.einsum('nihd,njhd->nhij', qc, kc) * decay[None]
        o = (jnp.einsum('nhij,njhe->nihe', a, vc)
             + jnp.einsum('nihd,nhde->nihe', qc * q_dec[None, :, :, None], S))
        S = c_dec[None, :, None, None] * S + jnp.einsum('njhd,njhe->nhde', kc * k_dec[None, :, :, None], vc)
        return S, o

    S, o = lax.scan(step, state.astype(jnp.float32), (to_chunks(q), to_chunks(k), to_chunks(v)))
    return o.swapaxes(0, 1).reshape(N, L, H, Dv), S


def _ret_mixer(h, pos, state, w_in, w_out):
    N, L, _ = h.shape
    z = h @ w_in
    q = _rope(z[..., :RET_QK].reshape(N, L, H_R, DK_R), pos)
    k = _rope(z[..., RET_QK:2 * RET_QK].reshape(N, L, H_R, DK_R), pos) * (DK_R ** -0.5)
    v = z[..., 2 * RET_QK:2 * RET_QK + RET_V].reshape(N, L, H_R, DV_R)
    gate = z[..., 2 * RET_QK + RET_V:]
    o, S = _retention(q, k, v, state)
    mu = jnp.mean(o, axis=-1, keepdims=True)
    var = jnp.mean(jnp.square(o - mu), axis=-1, keepdims=True)
    o = (o - mu) * lax.rsqrt(var + GN_EPS)
    y = (jax.nn.silu(gate.astype(jnp.float32)) * o.reshape(N, L, RET_V)).astype(h.dtype)
    return y @ w_out, S


def _mlp(h, w1, w2):
    a = jax.nn.relu(h @ w1)
    return (a * a) @ w2


def setup_inputs(seed: int = 0) -> dict:
    key = jax.random.key(seed)
    ks = jax.random.split(key, 15)

    def nrm(k, shape, scale):
        return jax.random.normal(k, shape, jnp.float32) * scale

    return {
        'x_prompt': nrm(ks[0], (BATCH, SEQ, D_MODEL), 1.0),
        'x_sample': nrm(ks[1], (DEC_BATCH, DEC_SEQ, D_MODEL), 1.0),
        'cache_dil_w128': nrm(ks[2], (N_EVEN, DEC_BATCH, min(DIL_PAIRS[0][0], PAST_LEN), 2, H_A, HEAD_DIM), 1.0),
        'cache_dil_w512': nrm(ks[3], (N_EVEN, DEC_BATCH, min(DIL_PAIRS[1][0], PAST_LEN), 2, H_A, HEAD_DIM), 1.0),
        'cache_dil_w2048': nrm(ks[4], (N_EVEN, DEC_BATCH, min(DIL_PAIRS[2][0], PAST_LEN), 2, H_A, HEAD_DIM), 1.0),
        'cache_swa': nrm(ks[5], (N_EVEN, DEC_BATCH, min(SWA_WINDOW, PAST_LEN), 2, H_BKV, HEAD_DIM), 1.0),
        'state_ret': nrm(ks[6], (N_ODD, DEC_BATCH, H_R, DK_R, DV_R), 0.5),
        'norm_gains': 1.0 + nrm(ks[7], (DEPTH, 4, D_MODEL), 0.05),
        'w_in_mix': nrm(ks[8], (N_EVEN, D_MODEL, MIX_IN), D_MODEL ** -0.5),
        'w_out_mix': nrm(ks[9], (N_EVEN, MIX_OUT, D_MODEL), MIX_OUT ** -0.5),
        'attn_sinks': nrm(ks[10], (N_EVEN, H_B), 1.0),
        'w_in_ret': nrm(ks[11], (N_ODD, D_MODEL, RET_IN), D_MODEL ** -0.5),
        'w_out_ret': nrm(ks[12], (N_ODD, RET_V, D_MODEL), RET_V ** -0.5),
        'w_ff1': nrm(ks[13], (DEPTH, D_MODEL, D_FF), D_MODEL ** -0.5),
        'w_ff2': nrm(ks[14], (DEPTH, D_FF, D_MODEL), D_FF ** -0.5),
    }


def reference(x_prompt, x_sample, cache_dil_w128, cache_dil_w512, cache_dil_w2048, cache_swa, state_ret,
              norm_gains, w_in_mix, w_out_mix, attn_sinks, w_in_ret, w_out_ret, w_ff1, w_ff2):
    xp, xs = x_prompt, x_sample
    dil_caches = (cache_dil_w128, cache_dil_w512, cache_dil_w2048)
    dil_p = [[] for _ in DIL_PAIRS]
    dil_s = [[] for _ in DIL_PAIRS]
    swa_p, swa_s, ret_p, ret_s = [], [], [], []
    pos_p = np.arange(xp.shape[1])
    pos_s = PAST_LEN + np.arange(xs.shape[1])
    for layer in range(DEPTH):
        gn = norm_gains[layer]
        hp = _rmsnorm(xp, gn[0])
        hs = _rmsnorm(xs, gn[0])
        if layer % 2 == 0:
            e = layer // 2
            mp, rows_p, sw_p = _ab_prompt(hp, pos_p, w_in_mix[e], w_out_mix[e], attn_sinks[e])
            ms, rows_s, sw_s = _ab_sample(hs, pos_s, [c[e] for c in dil_caches], cache_swa[e],
                                          w_in_mix[e], w_out_mix[e], attn_sinks[e])
            for g in range(N_DIL):
                dil_p[g].append(rows_p[g])
                dil_s[g].append(rows_s[g])
            swa_p.append(sw_p)
            swa_s.append(sw_s)
        else:
            o = layer // 2
            zero_state = jnp.zeros((xp.shape[0], H_R, DK_R, DV_R), jnp.float32)
            mp, st_p = _ret_mixer(hp, pos_p, zero_state, w_in_ret[o], w_out_ret[o])
            ms, st_s = _ret_mixer(hs, pos_s, state_ret[o], w_in_ret[o], w_out_ret[o])
            ret_p.append(st_p.astype(state_ret.dtype))
            ret_s.append(st_s.astype(state_ret.dtype))
        xp = xp + _rmsnorm(mp, gn[1])
        xs = xs + _rmsnorm(ms, gn[1])
        xp = xp + _rmsnorm(_mlp(_rmsnorm(xp, gn[2]), w_ff1[layer], w_ff2[layer]), gn[3])
        xs = xs + _rmsnorm(_mlp(_rmsnorm(xs, gn[2]), w_ff1[layer], w_ff2[layer]), gn[3])
    return (xp, xs,
            jnp.stack(dil_p[0]), jnp.stack(dil_s[0]),
            jnp.stack(dil_p[1]), jnp.stack(dil_s[1]),
            jnp.stack(dil_p[2]), jnp.stack(dil_s[2]),
            jnp.stack(swa_p), jnp.stack(swa_s),
            jnp.stack(ret_p), jnp.stack(ret_s))
```

```python
import functools

import numpy as np
import jax
import jax.numpy as jnp
from jax import lax
from jax.experimental import pallas as pl
from jax.experimental.pallas import tpu as pltpu

F32 = jnp.float32
BF16 = jnp.bfloat16

HEAD_DIM = 64
ROPE_HALF = HEAD_DIM // 2
N_HEADS = 8
HW = N_HEADS * HEAD_DIM
DIL_PAIRS = ((128, 1), (512, 4), (2048, 16))
N_DIL = len(DIL_PAIRS)
H_BKV = 2
G_B = N_HEADS // H_BKV
KVB = H_BKV * HEAD_DIM
SWA_WINDOW = 128
H_R = 4
DK_R = 256
DV_R = 512
RET_CHUNK = 128
BLOCK = 128
PAST_LEN = 8192
ROPE_THETA = 10000.0
NORM_EPS = 1e-6
GN_EPS = 1e-5
NEG_INF = -1e30
LANES = 128
VMEM_LIMIT = 56 * 1024 * 1024


def _params(sem):
    return pltpu.CompilerParams(dimension_semantics=sem, vmem_limit_bytes=VMEM_LIMIT)


def _rms(x, g):
    return x * lax.rsqrt(jnp.mean(x * x, axis=-1, keepdims=True) + NORM_EPS) * g


def _dot(a, b):
    return jnp.dot(a, b, preferred_element_type=F32)


def _dot_nt(a, b):
    return lax.dot_general(a, b, (((1,), (1,)), ((), ())), preferred_element_type=F32)


def _dot_tn(a, b):
    return lax.dot_general(a, b, (((0,), (0,)), ((), ())), preferred_element_type=F32)


def _rope_tables(pos, dh):
    half = dh // 2
    inv = ROPE_THETA ** (-np.arange(half, dtype=np.float64) * (2.0 / dh))
    ang = np.asarray(pos, np.float64)[:, None] * inv[None, :]
    c, s = np.cos(ang), np.sin(ang)
    if dh == HEAD_DIM:
        cos = np.tile(np.concatenate([c, c], axis=1), (1, LANES // dh))
        sin = np.tile(np.concatenate([-s, s], axis=1), (1, LANES // dh))
    else:
        cos, sin = c, s
    return jnp.asarray(cos, F32), jnp.asarray(sin, F32)


def _rope64(z, cos, sin):
    w = z.shape[1]
    reps = w // LANES
    lane = lax.broadcasted_iota(jnp.int32, z.shape, 1)
    first_half = (lane & (HEAD_DIM - 1)) < ROPE_HALF
    partner = jnp.where(first_half, pltpu.roll(z, w - ROPE_HALF, 1), pltpu.roll(z, ROPE_HALF, 1))
    if reps > 1:
        cos = jnp.concatenate([cos] * reps, axis=1)
        sin = jnp.concatenate([sin] * reps, axis=1)
    return z * cos + partner * sin


def _mix_proj_body(x_ref, g_ref, w_ref, cos_ref, sin_ref, qa_ref, kv0_ref, kv1_ref, kv2_ref, zb_ref):
    h = _rms(x_ref[...], g_ref[...]).astype(BF16)
    cos, sin = cos_ref[...], sin_ref[...]
    kv_refs = (kv0_ref, kv1_ref, kv2_ref)
    for g in range(N_DIL):
        base = g * 3 * HW
        qa_ref[:, g * HW:(g + 1) * HW] = _rope64(_dot(h, w_ref[:, base:base + HW]), cos, sin)
        kv_refs[g][:, :HW] = _rope64(_dot(h, w_ref[:, base + HW:base + 2 * HW]), cos, sin)
        kv_refs[g][:, HW:] = _dot(h, w_ref[:, base + 2 * HW:base + 3 * HW])
    base = N_DIL * 3 * HW
    zb_ref[:, :HW] = _rope64(_dot(h, w_ref[:, base:base + HW]), cos, sin)
    zb_ref[:, HW:HW + KVB] = _rope64(_dot(h, w_ref[:, base + HW:base + HW + KVB]), cos, sin)
    zb_ref[:, HW + KVB:] = _dot(h, w_ref[:, base + HW + KVB:base + HW + 2 * KVB])


def _ret_proj_body(x_ref, g_ref, w_ref, cos_ref, sin_ref, z_ref):
    h = _rms(x_ref[...], g_ref[...]).astype(BF16)
    cos, sin = cos_ref[...], sin_ref[...]
    half = DK_R // 2
    for part, scale in ((0, 1.0), (1, DK_R ** -0.5)):
        for hh in range(H_R):
            c0 = part * H_R * DK_R + hh * DK_R
            z = _dot(h, w_ref[:, c0:c0 + DK_R])
            z1, z2 = z[:, :half], z[:, half:]
            z_ref[:, c0:c0 + half] = (z1 * cos - z2 * sin) * scale
            z_ref[:, c0 + half:c0 + DK_R] = (z2 * cos + z1 * sin) * scale
    c0 = 2 * H_R * DK_R
    for j in range(2 * H_R):
        z_ref[:, c0 + j * DV_R:c0 + (j + 1) * DV_R] = _dot(h, w_ref[:, c0 + j * DV_R:c0 + (j + 1) * DV_R])


def _norm_proj(body, x, gain, w, cos, sin, out_widths, tm):
    m, d = x.shape
    tm = min(tm, m)
    assert m % tm == 0 and cos.shape[0] % tm == 0
    period = cos.shape[0] // tm
    n = w.shape[1]
    outs = pl.pallas_call(
        body,
        grid=(m // tm,),
        in_specs=[
            pl.BlockSpec((tm, d), lambda i: (i, 0)),
            pl.BlockSpec((1, d), lambda i: (0, 0)),
            pl.BlockSpec((d, n), lambda i: (0, 0), pipeline_mode=pl.Buffered(1)),
            pl.BlockSpec((tm, LANES), lambda i: (i % period, 0)),
            pl.BlockSpec((tm, LANES), lambda i: (i % period, 0)),
        ],
        out_specs=[pl.BlockSpec((tm, wd), lambda i: (i, 0)) for wd in out_widths],
        out_shape=[jax.ShapeDtypeStruct((m, wd), F32) for wd in out_widths],
        compiler_params=_params(("parallel",)),
    )(x, gain.reshape(1, d), w, cos, sin)
    return outs


def _band_heads(q_ref, kc_ref, vc_ref, kp_ref, vp_ref, lb, *, k_off, v_off, kv_group, min_back, sinks):
    row = lax.broadcasted_iota(jnp.int32, (BLOCK, BLOCK), 0)
    col = lax.broadcasted_iota(jnp.int32, (BLOCK, BLOCK), 1)
    mask_c = col <= row
    if kp_ref is not None:
        mask_p = jnp.logical_and(col - row >= min_back, lb > 0)
    for hq in range(N_HEADS):
        hk = hq // kv_group
        qh = (q_ref[:, hq * HEAD_DIM:(hq + 1) * HEAD_DIM] * (HEAD_DIM ** -0.5)).astype(BF16)
        ks = slice(k_off + hk * HEAD_DIM, k_off + (hk + 1) * HEAD_DIM)
        vs = slice(v_off + hk * HEAD_DIM, v_off + (hk + 1) * HEAD_DIM)
        s_c = jnp.where(mask_c, _dot_nt(qh, kc_ref[:, ks].astype(BF16)), NEG_INF)
        m = jnp.max(s_c, axis=-1, keepdims=True)
        if kp_ref is not None:
            s_p = jnp.where(mask_p, _dot_nt(qh, kp_ref[:, ks].astype(BF16)), NEG_INF)
            m = jnp.maximum(m, jnp.max(s_p, axis=-1, keepdims=True))
        if sinks is not None:
            m = jnp.maximum(m, sinks[hq])
        p_c = jnp.exp(s_c - m)
        l = jnp.sum(p_c, axis=-1, keepdims=True)
        o = _dot(p_c.astype(BF16), vc_ref[:, vs].astype(BF16))
        if kp_ref is not None:
            p_p = jnp.exp(s_p - m)
            l = l + jnp.sum(p_p, axis=-1, keepdims=True)
            o = o + _dot(p_p.astype(BF16), vp_ref[:, vs].astype(BF16))
        if sinks is not None:
            l = l + jnp.exp(sinks[hq] - m)
        yield hq, o / l, m + jnp.log(l)


def _dil_body(*refs, has_prev):
    if has_prev:
        q_ref, kc_ref, vc_ref, kp_ref, vp_ref, o_ref = refs
    else:
        q_ref, kc_ref, vc_ref, o_ref = refs
        kp_ref = vp_ref = None
    lb = pl.program_id(2)
    for hq, o, lse in _band_heads(q_ref, kc_ref, vc_ref, kp_ref, vp_ref, lb,
                                  k_off=0, v_off=0, kv_group=1, min_back=0, sinks=None):
        o_ref[:, hq * HEAD_DIM:(hq + 1) * HEAD_DIM] = o
        o_ref[:, HW + hq * HEAD_DIM:HW + (hq + 1) * HEAD_DIM] = jnp.broadcast_to(lse, (BLOCK, HEAD_DIM))


def _dilated_prompt(qa, kv, g, dil, nseq, seq):
    ls = seq // dil
    nb = ls // BLOCK
    has_prev = nb > 1
    qv = qa.reshape(nseq, ls, dil * N_DIL * HW)
    kvv = kv.reshape(nseq, ls, dil * 2 * HW)
    blk = (None, BLOCK, HW)
    in_specs = [
        pl.BlockSpec(blk, lambda n, r, b: (n, b, r * N_DIL + g)),
        pl.BlockSpec(blk, lambda n, r, b: (n, b, 2 * r)),
        pl.BlockSpec(blk, lambda n, r, b: (n, b, 2 * r + 1)),
    ]
    args = [qv, kvv, kvv]
    if has_prev:
        in_specs += [
            pl.BlockSpec(blk, lambda n, r, b: (n, jnp.maximum(b - 1, 0), 2 * r)),
            pl.BlockSpec(blk, lambda n, r, b: (n, jnp.maximum(b - 1, 0), 2 * r + 1)),
        ]
        args += [kvv, kvv]
    out = pl.pallas_call(
        functools.partial(_dil_body, has_prev=has_prev),
        grid=(nseq, dil, nb),
        in_specs=in_specs,
        out_specs=pl.BlockSpec((None, BLOCK, 2 * HW), lambda n, r, b: (n, b, r)),
        out_shape=jax.ShapeDtypeStruct((nseq, ls, dil * 2 * HW), F32),
        compiler_params=_params(("parallel", "parallel", "arbitrary")),
    )(*args)
    return out.reshape(nseq * seq, 2 * HW)


def _swa_body(sink_ref, q_ref, kvc_ref, kvp_ref, o_ref):
    lb = pl.program_id(1)
    sinks = [sink_ref[0, h] for h in range(N_HEADS)]
    for hq, o, _ in _band_heads(q_ref, kvc_ref, kvc_ref, kvp_ref, kvp_ref, lb,
                                k_off=0, v_off=KVB, kv_group=G_B, min_back=1, sinks=sinks):
        o_ref[:, hq * HEAD_DIM:(hq + 1) * HEAD_DIM] = o


def _swa_prompt(zb, sinks, nseq, seq):
    nb = seq // BLOCK
    zv = zb.reshape(nseq, seq, HW + 2 * KVB)
    kv_blk = (None, BLOCK, 2 * KVB)
    kv_col = HW // (2 * KVB)
    out = pl.pallas_call(
        _swa_body,
        grid=(nseq, nb),
        in_specs=[
            pl.BlockSpec(memory_space=pltpu.SMEM),
            pl.BlockSpec((None, BLOCK, HW), lambda n, b: (n, b, 0)),
            pl.BlockSpec(kv_blk, lambda n, b: (n, b, kv_col)),
            pl.BlockSpec(kv_blk, lambda n, b: (n, jnp.maximum(b - 1, 0), kv_col)),
        ],
        out_specs=pl.BlockSpec((None, BLOCK, HW), lambda n, b: (n, b, 0)),
        out_shape=jax.ShapeDtypeStruct((nseq, seq, HW), F32),
        compiler_params=_params(("parallel", "arbitrary")),
    )(sinks.reshape(1, N_HEADS), zv, zv, zv)
    return out.reshape(nseq * seq, HW)


def _sample_attn_body(qa_ref, kn0_ref, kn1_ref, kn2_ref, zb_ref, c0_ref, c1_ref, c2_ref, cs_ref, sink_ref,
                      y_ref, *, t_new):
    rows = t_new * N_HEADS
    r_i = lax.broadcasted_iota(jnp.int32, (rows, HW), 0)
    l_i = lax.broadcasted_iota(jnp.int32, (rows, HW), 1)
    own_head = (r_i & (N_HEADS - 1)) == (l_i >> 6)
    t_row = lax.broadcasted_iota(jnp.int32, (rows, BLOCK), 0) >> 3
    key = lax.broadcasted_iota(jnp.int32, (rows, BLOCK), 1)
    zero_rows = jnp.zeros((BLOCK - t_new, HW), BF16)

    def expand_q(q):
        qe = jnp.concatenate([jnp.broadcast_to(q[t:t + 1, :], (N_HEADS, HW)) for t in range(t_new)], axis=0)
        return jnp.where(own_head, qe * (HEAD_DIM ** -0.5), 0.0).astype(BF16)

    def pad_new(x):
        return jnp.concatenate([x.astype(BF16), zero_rows], axis=0)

    def softmax_parts(s_cache, s_new, mask_cache, mask_new, sink=None):
        s_cache = jnp.where(mask_cache, s_cache, NEG_INF)
        s_new = jnp.where(mask_new, s_new, NEG_INF)
        m = jnp.maximum(jnp.max(s_cache, axis=-1, keepdims=True), jnp.max(s_new, axis=-1, keepdims=True))
        if sink is not None:
            m = jnp.maximum(m, sink)
        p_cache = jnp.exp(s_cache - m)
        p_new = jnp.exp(s_new - m)
        l = jnp.sum(p_cache, axis=-1, keepdims=True) + jnp.sum(p_new, axis=-1, keepdims=True)
        if sink is not None:
            l = l + jnp.exp(sink - m)
        return m, l, p_cache.astype(BF16), p_new.astype(BF16)

    def own_head_rows(o):
        return jnp.sum(jnp.where(own_head, o, 0.0).reshape(t_new, N_HEADS, HW), axis=1)

    kn_refs = (kn0_ref, kn1_ref, kn2_ref)
    c_refs = (c0_ref, c1_ref, c2_ref)
    parts = []
    for g, (win, dil) in enumerate(DIL_PAIRS):
        qe = expand_q(qa_ref[:, g * HW:(g + 1) * HW])
        k_new = pad_new(kn_refs[g][:, :HW])
        v_new = pad_new(kn_refs[g][:, HW:])
        s_new = _dot_nt(qe, k_new)
        back = t_row - key
        mask_new = jnp.logical_and(jnp.logical_and(back >= 0, key < t_new), (back & (dil - 1)) == 0)
        c_ref = c_refs[g]
        if dil == 1:
            s_cache = _dot_nt(qe, c_ref[:, :HW].astype(BF16))
            mask_cache = key >= t_row
        else:
            s_cache = jnp.concatenate(
                [_dot_nt(qe[t * N_HEADS:(t + 1) * N_HEADS], c_ref[:, t * 2 * HW:t * 2 * HW + HW].astype(BF16))
                 for t in range(t_new)], axis=0)
            mask_cache = key >= 0
        m, l, p_cache, p_new = softmax_parts(s_cache, s_new, mask_cache, mask_new)
        if dil == 1:
            pv = _dot(p_cache, c_ref[:, HW:2 * HW].astype(BF16))
        else:
            pv = jnp.concatenate(
                [_dot(p_cache[t * N_HEADS:(t + 1) * N_HEADS], c_ref[:, t * 2 * HW + HW:(t + 1) * 2 * HW].astype(BF16))
                 for t in range(t_new)], axis=0)
        pv = pv + _dot(p_new, v_new)
        parts.append((m, l, pv))
    m_all = functools.reduce(jnp.maximum, [p[0] for p in parts])
    acc = jnp.zeros((rows, HW), F32)
    den = jnp.zeros((rows, 1), F32)
    for m, l, pv in parts:
        a = jnp.exp(m - m_all)
        acc = acc + a * pv
        den = den + a * l
    y_ref[:, :HW] = own_head_rows(acc / den)

    e_r = lax.broadcasted_iota(jnp.int32, (KVB, HW), 0)
    e_c = lax.broadcasted_iota(jnp.int32, (KVB, HW), 1)
    spread = jnp.where(jnp.logical_and((e_r >> 6) == (e_c >> 8), (e_r & 63) == (e_c & 63)), 1.0, 0.0).astype(BF16)

    def to_q_heads(x):
        return _dot(x, spread).astype(BF16)

    zero_kv = jnp.zeros((BLOCK - t_new, KVB), BF16)
    qe = expand_q(zb_ref[:, :HW])
    k_cache = to_q_heads(cs_ref[:, :KVB].astype(BF16))
    v_cache = to_q_heads(cs_ref[:, KVB:].astype(BF16))
    k_new = to_q_heads(jnp.concatenate([zb_ref[:, HW:HW + KVB].astype(BF16), zero_kv], axis=0))
    v_new = to_q_heads(jnp.concatenate([zb_ref[:, HW + KVB:].astype(BF16), zero_kv], axis=0))
    back = t_row - key
    m, l, p_cache, p_new = softmax_parts(
        _dot_nt(qe, k_cache), _dot_nt(qe, k_new),
        key >= t_row + 1 + (BLOCK - SWA_WINDOW), jnp.logical_and(back >= 0, key < t_new),
        sink=sink_ref[:, 0:1])
    ob = (_dot(p_cache, v_cache) + _dot(p_new, v_new)) / l
    y_ref[:, HW:] = own_head_rows(ob)


def _sample_attn(qa, kns, zb, caches, cache_swa, sinks, e, nreq, t_new):
    qv = qa.reshape(nreq, t_new, N_DIL * HW)
    knv = [k.reshape(nreq, t_new, 2 * HW) for k in kns]
    zbv = zb.reshape(nreq, t_new, HW + 2 * KVB)
    cv = []
    c_specs = []
    for (win, dil), c in zip(DIL_PAIRS, caches):
        wb = c.shape[2]
        assert wb == win and wb // dil == BLOCK
        cv.append(c.reshape(c.shape[0], nreq, BLOCK, dil * 2 * HW))
        width = min(dil, t_new) * 2 * HW
        c_specs.append(pl.BlockSpec((None, None, BLOCK, width), lambda b: (e, b, 0, 0)))
    assert cache_swa.shape[2] == BLOCK
    csv = cache_swa.reshape(cache_swa.shape[0], nreq, BLOCK, 2 * KVB)
    sink_rows = jnp.broadcast_to(jnp.tile(sinks, t_new)[:, None], (t_new * N_HEADS, LANES))
    tok = lambda wd: pl.BlockSpec((None, t_new, wd), lambda b: (b, 0, 0))
    out = pl.pallas_call(
        functools.partial(_sample_attn_body, t_new=t_new),
        grid=(nreq,),
        in_specs=[tok(N_DIL * HW), tok(2 * HW), tok(2 * HW), tok(2 * HW), tok(HW + 2 * KVB)] + c_specs + [
            pl.BlockSpec((None, None, BLOCK, 2 * KVB), lambda b: (e, b, 0, 0)),
            pl.BlockSpec((t_new * N_HEADS, LANES), lambda b: (0, 0)),
        ],
        out_specs=tok(2 * HW),
        out_shape=jax.ShapeDtypeStruct((nreq, t_new, 2 * HW), F32),
        compiler_params=_params(("parallel",)),
    )(qv, *knv, zbv, *cv, csv, sink_rows)
    return out.reshape(nreq * t_new, 2 * HW)


def _out_proj_body(*refs, merge):
    if merge:
        o0_ref, o1_ref, o2_ref, ob_ref, x_ref, g_ref, w_ref, out_ref = refs
        lses = [r[:, HW:] for r in (o0_ref, o1_ref, o2_ref)]
        m = jnp.maximum(jnp.maximum(lses[0], lses[1]), lses[2])
        es = [jnp.exp(l - m) for l in lses]
        num = es[0] * o0_ref[:, :HW] + es[1] * o1_ref[:, :HW] + es[2] * o2_ref[:, :HW]
        oa = num / (es[0] + es[1] + es[2])
        mo = _dot(oa.astype(BF16), w_ref[:HW, :]) + _dot(ob_ref[...].astype(BF16), w_ref[HW:, :])
    else:
        y_ref, x_ref, g_ref, w_ref, out_ref = refs
        mo = _dot(y_ref[...].astype(BF16), w_ref[...])
    out_ref[...] = x_ref[...] + _rms(mo, g_ref[...])


def _out_proj(ys, x, gain, w, tm, merge):
    m, d = x.shape
    tm = min(tm, m)
    assert m % tm == 0
    row = lambda wd: pl.BlockSpec((tm, wd), lambda i: (i, 0))
    return pl.pallas_call(
        functools.partial(_out_proj_body, merge=merge),
        grid=(m // tm,),
        in_specs=[row(y.shape[1]) for y in ys] + [
            row(d),
            pl.BlockSpec((1, d), lambda i: (0, 0)),
            pl.BlockSpec(w.shape, lambda i: (0, 0), pipeline_mode=pl.Buffered(1)),
        ],
        out_specs=row(d),
        out_shape=jax.ShapeDtypeStruct((m, d), F32),
        compiler_params=_params(("parallel",)),
    )(*ys, x, gain.reshape(1, d), w)


def _mlp_body(x_ref, g_pre_ref, g_post_ref, w1_ref, w2_ref, out_ref, h_ref, acc_ref):
    j = pl.program_id(1)

    @pl.when(j == 0)
    def _():
        h_ref[...] = _rms(x_ref[...], g_pre_ref[...]).astype(BF16)
        acc_ref[...] = jnp.zeros_like(acc_ref)

    a = jnp.maximum(_dot(h_ref[...], w1_ref[...]), 0.0)
    acc_ref[...] += _dot((a * a).astype(BF16), w2_ref[...])

    @pl.when(j == pl.num_programs(1) - 1)
    def _():
        out_ref[...] = x_ref[...] + _rms(acc_ref[...], g_post_ref[...])


def _mlp(x, g_pre, g_post, w1, w2, tm, tf):
    m, d = x.shape
    ff = w1.shape[1]
    tm = min(tm, m)
    assert m % tm == 0 and ff % tf == 0
    return pl.pallas_call(
        _mlp_body,
        grid=(m // tm, ff // tf),
        in_specs=[
            pl.BlockSpec((tm, d), lambda i, j: (i, 0)),
            pl.BlockSpec((1, d), lambda i, j: (0, 0)),
            pl.BlockSpec((1, d), lambda i, j: (0, 0)),
            pl.BlockSpec((d, tf), lambda i, j: (0, j)),
            pl.BlockSpec((tf, d), lambda i, j: (j, 0)),
        ],
        out_specs=pl.BlockSpec((tm, d), lambda i, j: (i, 0)),
        out_shape=jax.ShapeDtypeStruct((m, d), F32),
        scratch_shapes=[pltpu.VMEM((tm, d), BF16), pltpu.VMEM((tm, d), F32)],
        compiler_params=_params(("parallel", "arbitrary")),
    )(x, g_pre.reshape(1, d), g_post.reshape(1, d), w1, w2)


def _log_gammas():
    return jnp.asarray(np.log(1.0 - 2.0 ** (-5.0 - np.arange(H_R, dtype=np.float64))), F32)


def _group_norm_gate(o, gate):
    mu = jnp.mean(o, axis=-1, keepdims=True)
    var = jnp.mean(jnp.square(o - mu), axis=-1, keepdims=True)
    o = (o - mu) * lax.rsqrt(var + GN_EPS)
    return gate / (1.0 + jnp.exp(-gate)) * o


def _ret_prompt_body(lg_ref, q_ref, k_ref, v_ref, gate_ref, y_ref, st_ref, s_ref):
    c = pl.program_id(2)
    lg = lg_ref[pl.program_id(1)]

    @pl.when(c == 0)
    def _():
        s_ref[...] = jnp.zeros_like(s_ref)

    chunk = q_ref.shape[0]
    row = lax.broadcasted_iota(jnp.int32, (chunk, chunk), 0)
    col = lax.broadcasted_iota(jnp.int32, (chunk, chunk), 1)
    diff = (row - col).astype(F32)
    decay = jnp.where(diff >= 0, jnp.exp(jnp.maximum(diff, 0.0) * lg), 0.0)
    ii = lax.broadcasted_iota(jnp.int32, (chunk, 1), 0).astype(F32)
    q, k = q_ref[...], k_ref[...]
    v = v_ref[...].astype(BF16)
    a = _dot_nt(q.astype(BF16), k.astype(BF16)) * decay
    s_prev = s_ref[...]
    o = _dot(a.astype(BF16), v) + _dot((q * jnp.exp((ii + 1.0) * lg)).astype(BF16), s_prev.astype(BF16))
    kd = (k * jnp.exp((chunk - 1.0 - ii) * lg)).astype(BF16)
    s_new = jnp.exp(jnp.full((1, 1), chunk, F32) * lg) * s_prev + _dot_tn(kd, v)
    s_ref[...] = s_new
    y_ref[...] = _group_norm_gate(o, gate_ref[...])

    @pl.when(c == pl.num_programs(2) - 1)
    def _():
        st_ref[...] = s_new


def _ret_prompt(zr, nseq, seq):
    nc = seq // RET_CHUNK
    zv = zr.reshape(nseq, seq, zr.shape[1])
    kq = H_R
    y, st = pl.pallas_call(
        _ret_prompt_body,
        grid=(nseq, H_R, nc),
        in_specs=[
            pl.BlockSpec(memory_space=pltpu.SMEM),
            pl.BlockSpec((None, RET_CHUNK, DK_R), lambda n, h, c: (n, c, h)),
            pl.BlockSpec((None, RET_CHUNK, DK_R), lambda n, h, c: (n, c, kq + h)),
            pl.BlockSpec((None, RET_CHUNK, DV_R), lambda n, h, c: (n, c, kq + h)),
            pl.BlockSpec((None, RET_CHUNK, DV_R), lambda n, h, c: (n, c, 2 * kq + h)),
        ],
        out_specs=[
            pl.BlockSpec((None, RET_CHUNK, DV_R), lambda n, h, c: (n, c, h)),
            pl.BlockSpec((None, None, DK_R, DV_R), lambda n, h, c: (n, h, 0, 0)),
        ],
        out_shape=[
            jax.ShapeDtypeStruct((nseq, seq, H_R * DV_R), F32),
            jax.ShapeDtypeStruct((nseq, H_R, DK_R, DV_R), F32),
        ],
        scratch_shapes=[pltpu.VMEM((DK_R, DV_R), F32)],
        compiler_params=_params(("parallel", "parallel", "arbitrary")),
    )(_log_gammas(), zv, zv, zv, zv)
    return y.reshape(nseq * seq, H_R * DV_R), st


def _ret_sample_body(lg_ref, z_ref, s_in_ref, y_ref, s_out_ref, kpad_ref, vpad_ref, *, t_new):
    @pl.when(pl.program_id(0) == 0)
    def _():
        kpad_ref[...] = jnp.zeros_like(kpad_ref)
        vpad_ref[...] = jnp.zeros_like(vpad_ref)

    rows = 8
    row = lax.broadcasted_iota(jnp.int32, (rows, BLOCK), 0)
    col = lax.broadcasted_iota(jnp.int32, (rows, BLOCK), 1)
    diff = (row - col).astype(F32)
    ii = lax.broadcasted_iota(jnp.int32, (rows, 1), 0).astype(F32)
    ik = lax.broadcasted_iota(jnp.int32, (t_new, 1), 0).astype(F32)
    zq = jnp.zeros((rows - t_new, DK_R), F32)
    for h in range(H_R):
        lg = lg_ref[h]
        q = jnp.concatenate([z_ref[:, h * DK_R:(h + 1) * DK_R], zq], axis=0)
        k = z_ref[:, (H_R + h) * DK_R:(H_R + h + 1) * DK_R]
        v = z_ref[:, 2 * H_R * DK_R + h * DV_R:2 * H_R * DK_R + (h + 1) * DV_R]
        gate = z_ref[:, 2 * H_R * DK_R + (H_R + h) * DV_R:2 * H_R * DK_R + (H_R + h + 1) * DV_R]
        kpad_ref[0:t_new, :] = k
        vpad_ref[0:t_new, :] = v
        decay = jnp.where(diff >= 0, jnp.exp(jnp.maximum(diff, 0.0) * lg), 0.0)
        a = _dot_nt(q.astype(BF16), kpad_ref[...].astype(BF16)) * decay
        s_prev = s_in_ref[h]
        vb = vpad_ref[...].astype(BF16)
        o = _dot(a.astype(BF16), vb) + _dot((q * jnp.exp((ii + 1.0) * lg)).astype(BF16), s_prev.astype(BF16))
        kpad_ref[0:t_new, :] = k * jnp.exp((t_new - 1.0 - ik) * lg)
        upd = _dot_tn(kpad_ref[...].astype(BF16), vb)
        s_out_ref[h] = jnp.exp(jnp.full((1, 1), t_new, F32) * lg) * s_prev + upd
        y_ref[:, h * DV_R:(h + 1) * DV_R] = _group_norm_gate(o[0:t_new], gate)


def _ret_sample(zr, state, o, nreq, t_new):
    zv = zr.reshape(nreq, t_new, zr.shape[1])
    y, st = pl.pallas_call(
        functools.partial(_ret_sample_body, t_new=t_new),
        grid=(nreq,),
        in_specs=[
            pl.BlockSpec(memory_space=pltpu.SMEM),
            pl.BlockSpec((None, t_new, zr.shape[1]), lambda b: (b, 0, 0)),
            pl.BlockSpec((None, None, H_R, DK_R, DV_R), lambda b: (o, b, 0, 0, 0)),
        ],
        out_specs=[
            pl.BlockSpec((None, t_new, H_R * DV_R), lambda b: (b, 0, 0)),
            pl.BlockSpec((None, H_R, DK_R, DV_R), lambda b: (b, 0, 0, 0)),
        ],
        out_shape=[
            jax.ShapeDtypeStruct((nreq, t_new, H_R * DV_R), F32),
            jax.ShapeDtypeStruct((nreq, H_R, DK_R, DV_R), F32),
        ],
        scratch_shapes=[pltpu.VMEM((BLOCK, DK_R), F32), pltpu.VMEM((BLOCK, DV_R), F32)],
        compiler_params=_params(("arbitrary",)),
    )(_log_gammas(), zv, state)
    return y.reshape(nreq * t_new, H_R * DV_R), st


def kernel(x_prompt, x_sample, cache_dil_w128, cache_dil_w512, cache_dil_w2048, cache_swa, state_ret,
           norm_gains, w_in_mix, w_out_mix, attn_sinks, w_in_ret, w_out_ret, w_ff1, w_ff2):
    nseq, seq, d = x_prompt.shape
    nreq, t_new, _ = x_sample.shape
    depth = norm_gains.shape[0]
    assert seq % (BLOCK * DIL_PAIRS[-1][1]) == 0 and t_new <= DIL_PAIRS[1][1]
    xp = x_prompt.reshape(nseq * seq, d)
    xs = x_sample.reshape(nreq * t_new, d)
    dil_caches = (cache_dil_w128, cache_dil_w512, cache_dil_w2048)
    pos_p = np.arange(seq)
    pos_s = PAST_LEN + np.arange(t_new)
    rows_s = min(nreq * t_new, 512)
    rope_p = {dh: _rope_tables(pos_p, dh) for dh in (HEAD_DIM, DK_R)}
    rope_s = {dh: _rope_tables(np.tile(pos_s, rows_s // t_new), dh) for dh in (HEAD_DIM, DK_R)}
    mix_widths = (N_DIL * HW, 2 * HW, 2 * HW, 2 * HW, HW + 2 * KVB)

    dil_p = [[] for _ in DIL_PAIRS]
    dil_s = [[] for _ in DIL_PAIRS]
    swa_p, swa_s, ret_p, ret_s = [], [], [], []
    for layer in range(depth):
        gn = norm_gains[layer]
        if layer % 2 == 0:
            e = layer // 2
            w_in = w_in_mix[e].astype(BF16)
            w_out = w_out_mix[e].astype(BF16)
            qa_p, *kv_p, zb_p = _norm_proj(_mix_proj_body, xp, gn[0], w_in, *rope_p[HEAD_DIM], mix_widths, 256)
            qa_s, *kv_s, zb_s = _norm_proj(_mix_proj_body, xs, gn[0], w_in, *rope_s[HEAD_DIM], mix_widths, rows_s)
            o_groups = [_dilated_prompt(qa_p, kv_p[g], g, dil, nseq, seq) for g, (_, dil) in enumerate(DIL_PAIRS)]
            ob_p = _swa_prompt(zb_p, attn_sinks[e], nseq, seq)
            xp = _out_proj(o_groups + [ob_p], xp, gn[1], w_out, 512, merge=True)
            y_s = _sample_attn(qa_s, kv_s, zb_s, dil_caches, cache_swa, attn_sinks[e], e, nreq, t_new)
            xs = _out_proj([y_s], xs, gn[1], w_out, 512, merge=False)
            for g, (win, _) in enumerate(DIL_PAIRS):
                wb = min(win, seq)
                dil_p[g].append(kv_p[g].reshape(nseq, seq, 2, N_HEADS, HEAD_DIM)[:, seq - wb:])
                dil_s[g].append(kv_s[g].reshape(nreq, t_new, 2, N_HEADS, HEAD_DIM))
            wb = min(SWA_WINDOW, seq)
            swa_p.append(zb_p[:, HW:].reshape(nseq, seq, 2, H_BKV, HEAD_DIM)[:, seq - wb:])
            swa_s.append(zb_s[:, HW:].reshape(nreq, t_new, 2, H_BKV, HEAD_DIM))
        else:
            o = layer // 2
            w_in = w_in_ret[o].astype(BF16)
            w_out = w_out_ret[o].astype(BF16)
            width = (w_in.shape[1],)
            zr_p, = _norm_proj(_ret_proj_body, xp, gn[0], w_in, *rope_p[DK_R], width, 256)
            zr_s, = _norm_proj(_ret_proj_body, xs, gn[0], w_in, *rope_s[DK_R], width, rows_s)
            y_p, st_p = _ret_prompt(zr_p, nseq, seq)
            y_s, st_s = _ret_sample(zr_s, state_ret, o, nreq, t_new)
            xp = _out_proj([y_p], xp, gn[1], w_out, 512, merge=False)
            xs = _out_proj([y_s], xs, gn[1], w_out, 512, merge=False)
            ret_p.append(st_p)
            ret_s.append(st_s)
        w1 = w_ff1[layer].astype(BF16)
        w2 = w_ff2[layer].astype(BF16)
        xp = _mlp(xp, gn[2], gn[3], w1, w2, 1024, 512)
        xs = _mlp(xs, gn[2], gn[3], w1, w2, 512, 512)
    return (xp.reshape(nseq, seq, d), xs.reshape(nreq, t_new, d),
            jnp.stack(dil_p[0]), jnp.stack(dil_s[0]),
            jnp.stack(dil_p[1]), jnp.stack(dil_s[1]),
            jnp.stack(dil_p[2]), jnp.stack(dil_s[2]),
            jnp.stack(swa_p), jnp.stack(swa_s),
            jnp.stack(ret_p), jnp.stack(ret_s))
```

```python
import functools

import numpy as np
import jax
import jax.numpy as jnp
from jax import lax
from jax.experimental import pallas as pl
from jax.experimental.pallas import tpu as pltpu

F32 = jnp.float32
BF16 = jnp.bfloat16

HEAD_DIM = 64
ROPE_HALF = HEAD_DIM // 2
N_HEADS = 8
HW = N_HEADS * HEAD_DIM
DIL_PAIRS = ((128, 1), (512, 4), (2048, 16))
N_DIL = len(DIL_PAIRS)
H_BKV = 2
G_B = N_HEADS // H_BKV
KVB = H_BKV * HEAD_DIM
SWA_WINDOW = 128
H_R = 4
DK_R = 256
DV_R = 512
RET_CHUNK = 128
BLOCK = 128
PAST_LEN = 8192
ROPE_THETA = 10000.0
NORM_EPS = 1e-6
GN_EPS = 1e-5
NEG_INF = -1e30
LANES = 128
Q_PAD = 8
Q_SHIFT = Q_PAD.bit_length() - 1
HEAD_SHIFT = HEAD_DIM.bit_length() - 1
VMEM_LIMIT = 56 * 1024 * 1024


def _params(sem):
    return pltpu.CompilerParams(dimension_semantics=sem, vmem_limit_bytes=VMEM_LIMIT)


def _rms(x, g):
    return x * lax.rsqrt(jnp.mean(x * x, axis=-1, keepdims=True) + NORM_EPS) * g


def _dot(a, b):
    return jnp.dot(a, b, preferred_element_type=F32)


def _dot_nt(a, b):
    return lax.dot_general(a, b, (((1,), (1,)), ((), ())), preferred_element_type=F32)


def _dot_tn(a, b):
    return lax.dot_general(a, b, (((0,), (0,)), ((), ())), preferred_element_type=F32)


def _rope_tables(pos, dh):
    half = dh // 2
    inv = ROPE_THETA ** (-np.arange(half, dtype=np.float64) * (2.0 / dh))
    ang = np.asarray(pos, np.float64)[:, None] * inv[None, :]
    c, s = np.cos(ang), np.sin(ang)
    if dh == HEAD_DIM:
        cos = np.tile(np.concatenate([c, c], axis=1), (1, LANES // dh))
        sin = np.tile(np.concatenate([-s, s], axis=1), (1, LANES // dh))
    else:
        cos, sin = c, s
    return jnp.asarray(cos, F32), jnp.asarray(sin, F32)


def _rope64(z, cos, sin):
    w = z.shape[1]
    reps = w // LANES
    lane = lax.broadcasted_iota(jnp.int32, z.shape, 1)
    first_half = (lane & (HEAD_DIM - 1)) < ROPE_HALF
    partner = jnp.where(first_half, pltpu.roll(z, w - ROPE_HALF, 1), pltpu.roll(z, ROPE_HALF, 1))
    if reps > 1:
        cos = jnp.concatenate([cos] * reps, axis=1)
        sin = jnp.concatenate([sin] * reps, axis=1)
    return z * cos + partner * sin


def _mix_proj_body(x_ref, g_ref, w_ref, cos_ref, sin_ref, qa_ref, kv0_ref, kv1_ref, kv2_ref, zb_ref):
    h = _rms(x_ref[...], g_ref[...]).astype(BF16)
    cos, sin = cos_ref[...], sin_ref[...]
    kv_refs = (kv0_ref, kv1_ref, kv2_ref)
    for g in range(N_DIL):
        base = g * 3 * HW
        qa_ref[:, g * HW:(g + 1) * HW] = _rope64(_dot(h, w_ref[:, base:base + HW]), cos, sin)
        kv_refs[g][:, :HW] = _rope64(_dot(h, w_ref[:, base + HW:base + 2 * HW]), cos, sin)
        kv_refs[g][:, HW:] = _dot(h, w_ref[:, base + 2 * HW:base + 3 * HW])
    base = N_DIL * 3 * HW
    zb_ref[:, :HW] = _rope64(_dot(h, w_ref[:, base:base + HW]), cos, sin)
    zb_ref[:, HW:HW + KVB] = _rope64(_dot(h, w_ref[:, base + HW:base + HW + KVB]), cos, sin)
    zb_ref[:, HW + KVB:] = _dot(h, w_ref[:, base + HW + KVB:base + HW + 2 * KVB])


def _ret_proj_body(x_ref, g_ref, w_ref, cos_ref, sin_ref, z_ref):
    h = _rms(x_ref[...], g_ref[...]).astype(BF16)
    cos, sin = cos_ref[...], sin_ref[...]
    half = DK_R // 2
    for part, scale in ((0, 1.0), (1, DK_R ** -0.5)):
        for hh in range(H_R):
            c0 = part * H_R * DK_R + hh * DK_R
            z = _dot(h, w_ref[:, c0:c0 + DK_R])
            z1, z2 = z[:, :half], z[:, half:]
            z_ref[:, c0:c0 + half] = (z1 * cos - z2 * sin) * scale
            z_ref[:, c0 + half:c0 + DK_R] = (z2 * cos + z1 * sin) * scale
    c0 = 2 * H_R * DK_R
    for j in range(2 * H_R):
        z_ref[:, c0 + j * DV_R:c0 + (j + 1) * DV_R] = _dot(h, w_ref[:, c0 + j * DV_R:c0 + (j + 1) * DV_R])


def _norm_proj(body, x, gain, w, cos, sin, out_widths, tm):
    m, d = x.shape
    tm = min(tm, m)
    assert m % tm == 0 and cos.shape[0] % tm == 0
    period = cos.shape[0] // tm
    n = w.shape[1]
    outs = pl.pallas_call(
        body,
        grid=(m // tm,),
        in_specs=[
            pl.BlockSpec((tm, d), lambda i: (i, 0)),
            pl.BlockSpec((1, d), lambda i: (0, 0)),
            pl.BlockSpec((d, n), lambda i: (0, 0), pipeline_mode=pl.Buffered(1)),
            pl.BlockSpec((tm, LANES), lambda i: (i % period, 0)),
            pl.BlockSpec((tm, LANES), lambda i: (i % period, 0)),
        ],
        out_specs=[pl.BlockSpec((tm, wd), lambda i: (i, 0)) for wd in out_widths],
        out_shape=[jax.ShapeDtypeStruct((m, wd), F32) for wd in out_widths],
        compiler_params=_params(("parallel",)),
    )(x, gain.reshape(1, d), w, cos, sin)
    return outs


def _band_heads(q_ref, kc_ref, vc_ref, kp_ref, vp_ref, lb, *, k_off, v_off, kv_group, min_back, sinks):
    row = lax.broadcasted_iota(jnp.int32, (BLOCK, BLOCK), 0)
    col = lax.broadcasted_iota(jnp.int32, (BLOCK, BLOCK), 1)
    mask_c = col <= row
    if kp_ref is not None:
        mask_p = jnp.logical_and(col - row >= min_back, lb > 0)
    for hq in range(N_HEADS):
        hk = hq // kv_group
        qh = (q_ref[:, hq * HEAD_DIM:(hq + 1) * HEAD_DIM] * (HEAD_DIM ** -0.5)).astype(BF16)
        ks = slice(k_off + hk * HEAD_DIM, k_off + (hk + 1) * HEAD_DIM)
        vs = slice(v_off + hk * HEAD_DIM, v_off + (hk + 1) * HEAD_DIM)
        s_c = jnp.where(mask_c, _dot_nt(qh, kc_ref[:, ks].astype(BF16)), NEG_INF)
        if kp_ref is not None:
            s_p = jnp.where(mask_p, _dot_nt(qh, kp_ref[:, ks].astype(BF16)), NEG_INF)
            m = jnp.max(jnp.maximum(s_c, s_p), axis=-1, keepdims=True)
        else:
            m = jnp.max(s_c, axis=-1, keepdims=True)
        if sinks is not None:
            m = jnp.maximum(m, sinks[hq])
        p_c = jnp.exp(s_c - m)
        o = _dot(p_c.astype(BF16), vc_ref[:, vs].astype(BF16))
        if kp_ref is not None:
            p_p = jnp.exp(s_p - m)
            l = jnp.sum(p_c + p_p, axis=-1, keepdims=True)
            o = o + _dot(p_p.astype(BF16), vp_ref[:, vs].astype(BF16))
        else:
            l = jnp.sum(p_c, axis=-1, keepdims=True)
        if sinks is not None:
            l = l + jnp.exp(sinks[hq] - m)
        yield hq, o / l, m + jnp.log(l)


def _dil_body(*refs, has_prev):
    if has_prev:
        q_ref, kc_ref, vc_ref, kp_ref, vp_ref, o_ref = refs
    else:
        q_ref, kc_ref, vc_ref, o_ref = refs
        kp_ref = vp_ref = None
    lb = pl.program_id(2)
    for hq, o, lse in _band_heads(q_ref, kc_ref, vc_ref, kp_ref, vp_ref, lb,
                                  k_off=0, v_off=0, kv_group=1, min_back=0, sinks=None):
        o_ref[:, hq * HEAD_DIM:(hq + 1) * HEAD_DIM] = o
        o_ref[:, HW + hq * HEAD_DIM:HW + (hq + 1) * HEAD_DIM] = jnp.broadcast_to(lse, (BLOCK, HEAD_DIM))


def _dilated_prompt(qa, kv, g, dil, nseq, seq):
    ls = seq // dil
    nb = ls // BLOCK
    has_prev = nb > 1
    qv = qa.reshape(nseq, ls, dil * N_DIL * HW)
    kvv = kv.reshape(nseq, ls, dil * 2 * HW)
    blk = (None, BLOCK, HW)
    in_specs = [
        pl.BlockSpec(blk, lambda n, r, b: (n, b, r * N_DIL + g)),
        pl.BlockSpec(blk, lambda n, r, b: (n, b, 2 * r)),
        pl.BlockSpec(blk, lambda n, r, b: (n, b, 2 * r + 1)),
    ]
    args = [qv, kvv, kvv]
    if has_prev:
        in_specs += [
            pl.BlockSpec(blk, lambda n, r, b: (n, jnp.maximum(b - 1, 0), 2 * r)),
            pl.BlockSpec(blk, lambda n, r, b: (n, jnp.maximum(b - 1, 0), 2 * r + 1)),
        ]
        args += [kvv, kvv]
    out = pl.pallas_call(
        functools.partial(_dil_body, has_prev=has_prev),
        grid=(nseq, dil, nb),
        in_specs=in_specs,
        out_specs=pl.BlockSpec((None, BLOCK, 2 * HW), lambda n, r, b: (n, b, r)),
        out_shape=jax.ShapeDtypeStruct((nseq, ls, dil * 2 * HW), F32),
        compiler_params=_params(("parallel", "parallel", "arbitrary")),
    )(*args)
    return out.reshape(nseq * seq, 2 * HW)


def _swa_body(sink_ref, q_ref, kvc_ref, kvp_ref, o_ref):
    lb = pl.program_id(1)
    sinks = [sink_ref[0, h] for h in range(N_HEADS)]
    for hq, o, _ in _band_heads(q_ref, kvc_ref, kvc_ref, kvp_ref, kvp_ref, lb,
                                k_off=0, v_off=KVB, kv_group=G_B, min_back=1, sinks=sinks):
        o_ref[:, hq * HEAD_DIM:(hq + 1) * HEAD_DIM] = o


def _swa_prompt(zb, sinks, nseq, seq):
    nb = seq // BLOCK
    zv = zb.reshape(nseq, seq, HW + 2 * KVB)
    kv_blk = (None, BLOCK, 2 * KVB)
    kv_col = HW // (2 * KVB)
    out = pl.pallas_call(
        _swa_body,
        grid=(nseq, nb),
        in_specs=[
            pl.BlockSpec(memory_space=pltpu.SMEM),
            pl.BlockSpec((None, BLOCK, HW), lambda n, b: (n, b, 0)),
            pl.BlockSpec(kv_blk, lambda n, b: (n, b, kv_col)),
            pl.BlockSpec(kv_blk, lambda n, b: (n, jnp.maximum(b - 1, 0), kv_col)),
        ],
        out_specs=pl.BlockSpec((None, BLOCK, HW), lambda n, b: (n, b, 0)),
        out_shape=jax.ShapeDtypeStruct((nseq, seq, HW), F32),
        compiler_params=_params(("parallel", "arbitrary")),
    )(sinks.reshape(1, N_HEADS), zv, zv, zv)
    return out.reshape(nseq * seq, HW)


def _sample_attn_body(qa_ref, kn0_ref, kn1_ref, kn2_ref, zb_ref, c0_ref, c1_ref, c2_ref, cs_ref, sink_ref,
                      y_ref, pvn_ref, *, t_new):
    scale = HEAD_DIM ** -0.5
    r_i = lax.broadcasted_iota(jnp.int32, (N_HEADS * Q_PAD, HW), 0)
    l_i = lax.broadcasted_iota(jnp.int32, (N_HEADS * Q_PAD, HW), 1)
    own_head = (r_i >> Q_SHIFT) == (l_i >> HEAD_SHIFT)
    t_q = lax.broadcasted_iota(jnp.int32, (Q_PAD, 1), 0)
    new_key = lax.broadcasted_iota(jnp.int32, (Q_PAD, BLOCK), 1)

    def pad_rows(x, n):
        return jnp.concatenate([x, jnp.zeros((n - x.shape[0], x.shape[1]), x.dtype)], axis=0)

    def head_cols(ref, base, h):
        return ref[:, base + h * HEAD_DIM:base + (h + 1) * HEAD_DIM]

    kn_refs = (kn0_ref, kn1_ref, kn2_ref)
    c_refs = (c0_ref, c1_ref, c2_ref)
    stats = [[] for _ in range(N_HEADS)]
    for g, (win, dil) in enumerate(DIL_PAIRS):
        c_ref = c_refs[g]
        wb = c_ref.shape[-1]
        q_all = pad_rows(qa_ref[:, g * HW:(g + 1) * HW] * scale, Q_PAD)
        q_exp = jnp.where(own_head, jnp.concatenate([q_all] * N_HEADS, axis=0), 0.0).astype(BF16)
        s_new_all = _dot_nt(q_exp, pad_rows(kn_refs[g][:, :HW].astype(BF16), BLOCK))
        back = t_q - new_key
        mask_new = jnp.logical_and(jnp.logical_and(back >= 0, new_key < t_new), (back & (dil - 1)) == 0)
        ahead = lax.broadcasted_iota(jnp.int32, (Q_PAD, wb), 1) - t_q
        mask_cache = jnp.logical_and(ahead >= 0, (ahead & (dil - 1)) == 0)
        p_news = []
        partial = []
        for h in range(N_HEADS):
            qh = pad_rows(head_cols(qa_ref, g * HW, h) * scale, Q_PAD).astype(BF16)
            s_c = jnp.where(mask_cache, _dot(qh, c_ref[0, h].astype(BF16)), NEG_INF)
            s_n = jnp.where(mask_new, s_new_all[h * Q_PAD:(h + 1) * Q_PAD], NEG_INF)
            m = jnp.maximum(jnp.max(s_c, axis=-1, keepdims=True), jnp.max(s_n, axis=-1, keepdims=True))
            p_c = jnp.exp(s_c - m)
            p_n = jnp.exp(s_n - m)
            l = jnp.sum(p_c, axis=-1, keepdims=True) + jnp.sum(p_n, axis=-1, keepdims=True)
            partial.append((m, l, _dot_nt(p_c.astype(BF16), c_ref[1, h].astype(BF16))))
            p_news.append(p_n.astype(BF16))
        pvn_ref[...] = _dot(jnp.concatenate(p_news, axis=0), pad_rows(kn_refs[g][:, HW:].astype(BF16), BLOCK))
        for h, (m, l, pv) in enumerate(partial):
            pv = pv + pvn_ref[h * Q_PAD:(h + 1) * Q_PAD, h * HEAD_DIM:(h + 1) * HEAD_DIM]
            stats[h].append((m, l, pv))
    for h in range(N_HEADS):
        m_all = functools.reduce(jnp.maximum, [s[0] for s in stats[h]])
        acc = jnp.zeros((Q_PAD, HEAD_DIM), F32)
        den = jnp.zeros((Q_PAD, 1), F32)
        for m, l, pv in stats[h]:
            a = jnp.exp(m - m_all)
            acc = acc + a * pv
            den = den + a * l
        y_ref[:, h * HEAD_DIM:(h + 1) * HEAD_DIM] = (acc / den)[0:t_new]

    t_g = lax.broadcasted_iota(jnp.int32, (G_B * Q_PAD, 1), 0) & (Q_PAD - 1)
    key_g = lax.broadcasted_iota(jnp.int32, (G_B * Q_PAD, BLOCK), 1)
    mask_cache = key_g >= t_g + 1 + (BLOCK - SWA_WINDOW)
    mask_new = jnp.logical_and(t_g - key_g >= 0, key_g < t_new)
    for hk in range(H_BKV):
        qs = jnp.concatenate([pad_rows(head_cols(zb_ref, 0, hk * G_B + gq) * scale, Q_PAD) for gq in range(G_B)],
                             axis=0).astype(BF16)
        k_new = pad_rows(head_cols(zb_ref, HW, hk).astype(BF16), BLOCK)
        v_new = pad_rows(head_cols(zb_ref, HW + KVB, hk).astype(BF16), BLOCK)
        s_c = jnp.where(mask_cache, _dot(qs, cs_ref[0, hk].astype(BF16)), NEG_INF)
        s_n = jnp.where(mask_new, _dot_nt(qs, k_new), NEG_INF)
        sink = sink_ref[hk * G_B * Q_PAD:(hk + 1) * G_B * Q_PAD, 0:1]
        m = jnp.maximum(jnp.maximum(jnp.max(s_c, axis=-1, keepdims=True), jnp.max(s_n, axis=-1, keepdims=True)), sink)
        p_c = jnp.exp(s_c - m)
        p_n = jnp.exp(s_n - m)
        l = jnp.sum(p_c, axis=-1, keepdims=True) + jnp.sum(p_n, axis=-1, keepdims=True) + jnp.exp(sink - m)
        o = (_dot_nt(p_c.astype(BF16), cs_ref[1, hk].astype(BF16)) + _dot(p_n.astype(BF16), v_new)) / l
        for gq in range(G_B):
            hq = hk * G_B + gq
            y_ref[:, HW + hq * HEAD_DIM:HW + (hq + 1) * HEAD_DIM] = o[gq * Q_PAD:gq * Q_PAD + t_new]


def _sample_attn(qa, kns, zb, caches, cache_swa, sinks, e, nreq, t_new):
    assert t_new <= Q_PAD
    qv = qa.reshape(nreq, t_new, N_DIL * HW)
    knv = [k.reshape(nreq, t_new, 2 * HW) for k in kns]
    zbv = zb.reshape(nreq, t_new, HW + 2 * KVB)
    to_native = lambda c: jnp.transpose(c, (0, 1, 3, 4, 5, 2))
    cv = []
    c_specs = []
    for (win, dil), c in zip(DIL_PAIRS, caches):
        wb = c.shape[2]
        assert wb == win and wb // dil == BLOCK
        cv.append(to_native(c))
        c_specs.append(pl.BlockSpec((None, None, 2, N_HEADS, HEAD_DIM, wb), lambda b: (e, b, 0, 0, 0, 0)))
    assert cache_swa.shape[2] == BLOCK
    sink_rows = jnp.broadcast_to(jnp.repeat(sinks, Q_PAD)[:, None], (N_HEADS * Q_PAD, LANES))
    tok = lambda wd: pl.BlockSpec((None, t_new, wd), lambda b: (b, 0, 0))
    out = pl.pallas_call(
        functools.partial(_sample_attn_body, t_new=t_new),
        grid=(nreq,),
        in_specs=[tok(N_DIL * HW), tok(2 * HW), tok(2 * HW), tok(2 * HW), tok(HW + 2 * KVB)] + c_specs + [
            pl.BlockSpec((None, None, 2, H_BKV, HEAD_DIM, BLOCK), lambda b: (e, b, 0, 0, 0, 0)),
            pl.BlockSpec((N_HEADS * Q_PAD, LANES), lambda b: (0, 0)),
        ],
        out_specs=tok(2 * HW),
        out_shape=jax.ShapeDtypeStruct((nreq, t_new, 2 * HW), F32),
        scratch_shapes=[pltpu.VMEM((N_HEADS * Q_PAD, HW), F32)],
        compiler_params=_params(("parallel",)),
    )(qv, *knv, zbv, *cv, to_native(cache_swa), sink_rows)
    return out.reshape(nreq * t_new, 2 * HW)


def _out_proj_body(*refs, merge):
    if merge:
        o0_ref, o1_ref, o2_ref, ob_ref, x_ref, g_ref, w_ref, out_ref = refs
        lses = [r[:, HW:] for r in (o0_ref, o1_ref, o2_ref)]
        m = jnp.maximum(jnp.maximum(lses[0], lses[1]), lses[2])
        es = [jnp.exp(l - m) for l in lses]
        num = es[0] * o0_ref[:, :HW] + es[1] * o1_ref[:, :HW] + es[2] * o2_ref[:, :HW]
        oa = num / (es[0] + es[1] + es[2])
        mo = _dot(oa.astype(BF16), w_ref[:HW, :]) + _dot(ob_ref[...].astype(BF16), w_ref[HW:, :])
    else:
        y_ref, x_ref, g_ref, w_ref, out_ref = refs
        mo = _dot(y_ref[...].astype(BF16), w_ref[...])
    out_ref[...] = x_ref[...] + _rms(mo, g_ref[...])


def _out_proj(ys, x, gain, w, tm, merge):
    m, d = x.shape
    tm = min(tm, m)
    assert m % tm == 0
    row = lambda wd: pl.BlockSpec((tm, wd), lambda i: (i, 0))
    return pl.pallas_call(
        functools.partial(_out_proj_body, merge=merge),
        grid=(m // tm,),
        in_specs=[row(y.shape[1]) for y in ys] + [
            row(d),
            pl.BlockSpec((1, d), lambda i: (0, 0)),
            pl.BlockSpec(w.shape, lambda i: (0, 0), pipeline_mode=pl.Buffered(1)),
        ],
        out_specs=row(d),
        out_shape=jax.ShapeDtypeStruct((m, d), F32),
        compiler_params=_params(("parallel",)),
    )(*ys, x, gain.reshape(1, d), w)


def _mlp_body(x_ref, g_pre_ref, g_post_ref, w1_ref, w2_ref, out_ref, h_ref, acc_ref):
    j = pl.program_id(1)

    @pl.when(j == 0)
    def _():
        h_ref[...] = _rms(x_ref[...], g_pre_ref[...]).astype(BF16)
        acc_ref[...] = jnp.zeros_like(acc_ref)

    a = jnp.maximum(_dot(h_ref[...], w1_ref[...]), 0.0)
    acc_ref[...] += _dot((a * a).astype(BF16), w2_ref[...])

    @pl.when(j == pl.num_programs(1) - 1)
    def _():
        out_ref[...] = x_ref[...] + _rms(acc_ref[...], g_post_ref[...])


def _mlp(x, g_pre, g_post, w1, w2, tm, tf):
    m, d = x.shape
    ff = w1.shape[1]
    tm = min(tm, m)
    assert m % tm == 0 and ff % tf == 0
    return pl.pallas_call(
        _mlp_body,
        grid=(m // tm, ff // tf),
        in_specs=[
            pl.BlockSpec((tm, d), lambda i, j: (i, 0)),
            pl.BlockSpec((1, d), lambda i, j: (0, 0)),
            pl.BlockSpec((1, d), lambda i, j: (0, 0)),
            pl.BlockSpec((d, tf), lambda i, j: (0, j)),
            pl.BlockSpec((tf, d), lambda i, j: (j, 0)),
        ],
        out_specs=pl.BlockSpec((tm, d), lambda i, j: (i, 0)),
        out_shape=jax.ShapeDtypeStruct((m, d), F32),
        scratch_shapes=[pltpu.VMEM((tm, d), BF16), pltpu.VMEM((tm, d), F32)],
        compiler_params=_params(("parallel", "arbitrary")),
    )(x, g_pre.reshape(1, d), g_post.reshape(1, d), w1, w2)


def _log_gammas():
    return jnp.asarray(np.log(1.0 - 2.0 ** (-5.0 - np.arange(H_R, dtype=np.float64))), F32)


def _group_norm_gate(o, gate):
    mu = jnp.mean(o, axis=-1, keepdims=True)
    var = jnp.mean(jnp.square(o - mu), axis=-1, keepdims=True)
    o = (o - mu) * lax.rsqrt(var + GN_EPS)
    return gate / (1.0 + jnp.exp(-gate)) * o


def _ret_prompt_body(lg_ref, q_ref, k_ref, v_ref, gate_ref, y_ref, st_ref, s_ref):
    c = pl.program_id(2)
    lg = lg_ref[pl.program_id(1)]

    @pl.when(c == 0)
    def _():
        s_ref[...] = jnp.zeros_like(s_ref)

    chunk = q_ref.shape[0]
    row = lax.broadcasted_iota(jnp.int32, (chunk, chunk), 0)
    col = lax.broadcasted_iota(jnp.int32, (chunk, chunk), 1)
    diff = (row - col).astype(F32)
    decay = jnp.where(diff >= 0, jnp.exp(jnp.maximum(diff, 0.0) * lg), 0.0)
    ii = lax.broadcasted_iota(jnp.int32, (chunk, 1), 0).astype(F32)
    q, k = q_ref[...], k_ref[...]
    v = v_ref[...].astype(BF16)
    a = _dot_nt(q.astype(BF16), k.astype(BF16)) * decay
    s_prev = s_ref[...]
    o = _dot(a.astype(BF16), v) + _dot((q * jnp.exp((ii + 1.0) * lg)).astype(BF16), s_prev.astype(BF16))
    kd = (k * jnp.exp((chunk - 1.0 - ii) * lg)).astype(BF16)
    s_new = jnp.exp(jnp.full((1, 1), chunk, F32) * lg) * s_prev + _dot_tn(kd, v)
    s_ref[...] = s_new
    y_ref[...] = _group_norm_gate(o, gate_ref[...])

    @pl.when(c == pl.num_programs(2) - 1)
    def _():
        st_ref[...] = s_new


def _ret_prompt(zr, nseq, seq):
    nc = seq // RET_CHUNK
    zv = zr.reshape(nseq, seq, zr.shape[1])
    kq = H_R
    y, st = pl.pallas_call(
        _ret_prompt_body,
        grid=(nseq, H_R, nc),
        in_specs=[
            pl.BlockSpec(memory_space=pltpu.SMEM),
            pl.BlockSpec((None, RET_CHUNK, DK_R), lambda n, h, c: (n, c, h)),
            pl.BlockSpec((None, RET_CHUNK, DK_R), lambda n, h, c: (n, c, kq + h)),
            pl.BlockSpec((None, RET_CHUNK, DV_R), lambda n, h, c: (n, c, kq + h)),
            pl.BlockSpec((None, RET_CHUNK, DV_R), lambda n, h, c: (n, c, 2 * kq + h)),
        ],
        out_specs=[
            pl.BlockSpec((None, RET_CHUNK, DV_R), lambda n, h, c: (n, c, h)),
            pl.BlockSpec((None, None, DK_R, DV_R), lambda n, h, c: (n, h, 0, 0)),
        ],
        out_shape=[
            jax.ShapeDtypeStruct((nseq, seq, H_R * DV_R), F32),
            jax.ShapeDtypeStruct((nseq, H_R, DK_R, DV_R), F32),
        ],
        scratch_shapes=[pltpu.VMEM((DK_R, DV_R), F32)],
        compiler_params=_params(("parallel", "parallel", "arbitrary")),
    )(_log_gammas(), zv, zv, zv, zv)
    return y.reshape(nseq * seq, H_R * DV_R), st


def _ret_sample_body(*refs, t_new, has_prev):
    lg_ref, z_ref, s_in_ref = refs[:3]
    y_ref, s_out_ref, kpad_ref, vpad_ref = refs[4:] if has_prev else refs[3:]

    @pl.when(pl.program_id(0) == 0)
    def _():
        kpad_ref[...] = jnp.zeros_like(kpad_ref)
        vpad_ref[...] = jnp.zeros_like(vpad_ref)

    rows = 8
    row = lax.broadcasted_iota(jnp.int32, (rows, BLOCK), 0)
    col = lax.broadcasted_iota(jnp.int32, (rows, BLOCK), 1)
    diff = (row - col).astype(F32)
    ii = lax.broadcasted_iota(jnp.int32, (rows, 1), 0).astype(F32)
    ik = lax.broadcasted_iota(jnp.int32, (t_new, 1), 0).astype(F32)
    zq = jnp.zeros((rows - t_new, DK_R), F32)
    for h in range(H_R):
        lg = lg_ref[h]
        q = jnp.concatenate([z_ref[:, h * DK_R:(h + 1) * DK_R], zq], axis=0)
        k = z_ref[:, (H_R + h) * DK_R:(H_R + h + 1) * DK_R]
        v = z_ref[:, 2 * H_R * DK_R + h * DV_R:2 * H_R * DK_R + (h + 1) * DV_R]
        gate = z_ref[:, 2 * H_R * DK_R + (H_R + h) * DV_R:2 * H_R * DK_R + (H_R + h + 1) * DV_R]
        kpad_ref[0:t_new, :] = k
        vpad_ref[0:t_new, :] = v
        decay = jnp.where(diff >= 0, jnp.exp(jnp.maximum(diff, 0.0) * lg), 0.0)
        a = _dot_nt(q.astype(BF16), kpad_ref[...].astype(BF16)) * decay
        s_prev = s_in_ref[h]
        vb = vpad_ref[...].astype(BF16)
        o = _dot(a.astype(BF16), vb) + _dot((q * jnp.exp((ii + 1.0) * lg)).astype(BF16), s_prev.astype(BF16))
        kpad_ref[0:t_new, :] = k * jnp.exp((t_new - 1.0 - ik) * lg)
        upd = _dot_tn(kpad_ref[...].astype(BF16), vb)
        s_out_ref[h] = jnp.exp(jnp.full((1, 1), t_new, F32) * lg) * s_prev + upd
        y_ref[:, h * DV_R:(h + 1) * DV_R] = _group_norm_gate(o[0:t_new], gate)


def _ret_sample(zr, state, new_state, o, nreq, t_new):
    zv = zr.reshape(nreq, t_new, zr.shape[1])
    has_prev = new_state is not None
    state_blk = pl.BlockSpec((None, None, H_R, DK_R, DV_R), lambda b: (o, b, 0, 0, 0))
    in_specs = [
        pl.BlockSpec(memory_space=pltpu.SMEM),
        pl.BlockSpec((None, t_new, zr.shape[1]), lambda b: (b, 0, 0)),
        state_blk,
    ]
    args = [_log_gammas(), zv, state]
    if has_prev:
        in_specs.append(pl.BlockSpec(memory_space=pl.ANY))
        args.append(new_state)
    y, st = pl.pallas_call(
        functools.partial(_ret_sample_body, t_new=t_new, has_prev=has_prev),
        grid=(nreq,),
        in_specs=in_specs,
        out_specs=[pl.BlockSpec((None, t_new, H_R * DV_R), lambda b: (b, 0, 0)), state_blk],
        out_shape=[
            jax.ShapeDtypeStruct((nreq, t_new, H_R * DV_R), F32),
            jax.ShapeDtypeStruct(state.shape, F32),
        ],
        input_output_aliases={3: 1} if has_prev else {},
        scratch_shapes=[pltpu.VMEM((BLOCK, DK_R), F32), pltpu.VMEM((BLOCK, DV_R), F32)],
        compiler_params=_params(("arbitrary",)),
    )(*args)
    return y.reshape(nreq * t_new, H_R * DV_R), st


def kernel(x_prompt, x_sample, cache_dil_w128, cache_dil_w512, cache_dil_w2048, cache_swa, state_ret,
           norm_gains, w_in_mix, w_out_mix, attn_sinks, w_in_ret, w_out_ret, w_ff1, w_ff2):
    nseq, seq, d = x_prompt.shape
    nreq, t_new, _ = x_sample.shape
    depth = norm_gains.shape[0]
    assert seq % (BLOCK * DIL_PAIRS[-1][1]) == 0 and t_new <= DIL_PAIRS[1][1]
    xp = x_prompt.reshape(nseq * seq, d)
    xs = x_sample.reshape(nreq * t_new, d)
    dil_caches = (cache_dil_w128, cache_dil_w512, cache_dil_w2048)
    pos_p = np.arange(seq)
    pos_s = PAST_LEN + np.arange(t_new)
    rows_s = min(nreq * t_new, 512)
    rope_p = {dh: _rope_tables(pos_p, dh) for dh in (HEAD_DIM, DK_R)}
    rope_s = {dh: _rope_tables(np.tile(pos_s, rows_s // t_new), dh) for dh in (HEAD_DIM, DK_R)}
    mix_widths = (N_DIL * HW, 2 * HW, 2 * HW, 2 * HW, HW + 2 * KVB)

    dil_p = [[] for _ in DIL_PAIRS]
    dil_s = [[] for _ in DIL_PAIRS]
    swa_p, swa_s, ret_p = [], [], []
    ret_s = None
    for layer in range(depth):
        gn = norm_gains[layer]
        if layer % 2 == 0:
            e = layer // 2
            w_in = w_in_mix[e].astype(BF16)
            w_out = w_out_mix[e].astype(BF16)
            qa_p, *kv_p, zb_p = _norm_proj(_mix_proj_body, xp, gn[0], w_in, *rope_p[HEAD_DIM], mix_widths, 256)
            qa_s, *kv_s, zb_s = _norm_proj(_mix_proj_body, xs, gn[0], w_in, *rope_s[HEAD_DIM], mix_widths, rows_s)
            o_groups = [_dilated_prompt(qa_p, kv_p[g], g, dil, nseq, seq) for g, (_, dil) in enumerate(DIL_PAIRS)]
            ob_p = _swa_prompt(zb_p, attn_sinks[e], nseq, seq)
            xp = _out_proj(o_groups + [ob_p], xp, gn[1], w_out, 512, merge=True)
            y_s = _sample_attn(qa_s, kv_s, zb_s, dil_caches, cache_swa, attn_sinks[e], e, nreq, t_new)
            xs = _out_proj([y_s], xs, gn[1], w_out, 512, merge=False)
            for g, (win, _) in enumerate(DIL_PAIRS):
                wb = min(win, seq)
                dil_p[g].append(kv_p[g].reshape(nseq, seq, 2, N_HEADS, HEAD_DIM)[:, seq - wb:])
                dil_s[g].append(kv_s[g].reshape(nreq, t_new, 2, N_HEADS, HEAD_DIM))
            wb = min(SWA_WINDOW, seq)
            swa_p.append(zb_p[:, HW:].reshape(nseq, seq, 2, H_BKV, HEAD_DIM)[:, seq - wb:])
            swa_s.append(zb_s[:, HW:].reshape(nreq, t_new, 2, H_BKV, HEAD_DIM))
        else:
            o = layer // 2
            w_in = w_in_ret[o].astype(BF16)
            w_out = w_out_ret[o].astype(BF16)
            width = (w_in.shape[1],)
            zr_p, = _norm_proj(_ret_proj_body, xp, gn[0], w_in, *rope_p[DK_R], width, 256)
            zr_s, = _norm_proj(_ret_proj_body, xs, gn[0], w_in, *rope_s[DK_R], width, rows_s)
            y_p, st_p = _ret_prompt(zr_p, nseq, seq)
            y_s, ret_s = _ret_sample(zr_s, state_ret, ret_s, o, nreq, t_new)
            xp = _out_proj([y_p], xp, gn[1], w_out, 512, merge=False)
            xs = _out_proj([y_s], xs, gn[1], w_out, 512, merge=False)
            ret_p.append(st_p)
        w1 = w_ff1[layer].astype(BF16)
        w2 = w_ff2[layer].astype(BF16)
        xp = _mlp(xp, gn[2], gn[3], w1, w2, 1024, 512)
        xs = _mlp(xs, gn[2], gn[3], w1, w2, 512, 512)
    return (xp.reshape(nseq, seq, d), xs.reshape(nreq, t_new, d),
            jnp.stack(dil_p[0]), jnp.stack(dil_s[0]),
            jnp.stack(dil_p[1]), jnp.stack(dil_s[1]),
            jnp.stack(dil_p[2]), jnp.stack(dil_s[2]),
            jnp.stack(swa_p), jnp.stack(swa_s),
            jnp.stack(ret_p), ret_s)
```

```python
import functools

import numpy as np
import jax
import jax.numpy as jnp
from jax import lax
from jax.experimental import pallas as pl
from jax.experimental.pallas import tpu as pltpu

F32 = jnp.float32
BF16 = jnp.bfloat16

HEAD_DIM = 64
ROPE_HALF = HEAD_DIM // 2
N_HEADS = 8
HW = N_HEADS * HEAD_DIM
DIL_PAIRS = ((128, 1), (512, 4), (2048, 16))
N_DIL = len(DIL_PAIRS)
H_BKV = 2
G_B = N_HEADS // H_BKV
KVB = H_BKV * HEAD_DIM
SWA_WINDOW = 128
H_R = 4
DK_R = 256
DV_R = 512
RET_CHUNK = 128
BLOCK = 128
PAST_LEN = 8192
ROPE_THETA = 10000.0
NORM_EPS = 1e-6
GN_EPS = 1e-5
NEG_INF = -1e30
LANES = 128
BLOCKS_IN_FLIGHT = 3
Q_PAD = 8
Q_SHIFT = Q_PAD.bit_length() - 1
HEAD_SHIFT = HEAD_DIM.bit_length() - 1
VMEM_LIMIT = 56 * 1024 * 1024


def _swa_slot(hq):
    return (hq % G_B) * H_BKV + hq // G_B


def _slot_order(w, axis, base):
    heads = sorted(range(N_HEADS), key=_swa_slot)
    piece = lambda lo, hi: lax.slice_in_dim(w, lo, hi, axis=axis)
    return jnp.concatenate(
        [piece(0, base)] + [piece(base + h * HEAD_DIM, base + (h + 1) * HEAD_DIM) for h in heads]
        + [piece(base + HW, w.shape[axis])], axis=axis)


def _params(sem):
    return pltpu.CompilerParams(dimension_semantics=sem, vmem_limit_bytes=VMEM_LIMIT)


def _rms(x, g):
    return x * lax.rsqrt(jnp.mean(x * x, axis=-1, keepdims=True) + NORM_EPS) * g


def _dot(a, b):
    return jnp.dot(a, b, preferred_element_type=F32)


def _dot_nt(a, b):
    return lax.dot_general(a, b, (((1,), (1,)), ((), ())), preferred_element_type=F32)


def _dot_tn(a, b):
    return lax.dot_general(a, b, (((0,), (0,)), ((), ())), preferred_element_type=F32)


def _rope_tables(pos, dh):
    half = dh // 2
    inv = ROPE_THETA ** (-np.arange(half, dtype=np.float64) * (2.0 / dh))
    ang = np.asarray(pos, np.float64)[:, None] * inv[None, :]
    c, s = np.cos(ang), np.sin(ang)
    if dh == HEAD_DIM:
        cos = np.tile(np.concatenate([c, c], axis=1), (1, LANES // dh))
        sin = np.tile(np.concatenate([-s, s], axis=1), (1, LANES // dh))
    else:
        cos, sin = c, s
    return jnp.asarray(cos, F32), jnp.asarray(sin, F32)


def _rope64(z, cos, sin):
    w = z.shape[1]
    reps = w // LANES
    lane = lax.broadcasted_iota(jnp.int32, z.shape, 1)
    first_half = (lane & (HEAD_DIM - 1)) < ROPE_HALF
    partner = jnp.where(first_half, pltpu.roll(z, w - ROPE_HALF, 1), pltpu.roll(z, ROPE_HALF, 1))
    if reps > 1:
        cos = jnp.concatenate([cos] * reps, axis=1)
        sin = jnp.concatenate([sin] * reps, axis=1)
    return z * cos + partner * sin


def _mix_proj_body(x_ref, g_ref, w_ref, cos_ref, sin_ref, qa_ref, kv0_ref, kv1_ref, kv2_ref, zb_ref):
    h = _rms(x_ref[...], g_ref[...]).astype(BF16)
    cos, sin = cos_ref[...], sin_ref[...]
    kv_refs = (kv0_ref, kv1_ref, kv2_ref)
    for g in range(N_DIL):
        base = g * 3 * HW
        qa_ref[:, g * HW:(g + 1) * HW] = _rope64(_dot(h, w_ref[:, base:base + HW]), cos, sin)
        kv_refs[g][:, :HW] = _rope64(_dot(h, w_ref[:, base + HW:base + 2 * HW]), cos, sin)
        kv_refs[g][:, HW:] = _dot(h, w_ref[:, base + 2 * HW:base + 3 * HW])
    base = N_DIL * 3 * HW
    zb_ref[:, :HW] = _rope64(_dot(h, w_ref[:, base:base + HW]), cos, sin)
    zb_ref[:, HW:HW + KVB] = _rope64(_dot(h, w_ref[:, base + HW:base + HW + KVB]), cos, sin)
    zb_ref[:, HW + KVB:] = _dot(h, w_ref[:, base + HW + KVB:base + HW + 2 * KVB])


def _ret_proj_body(x_ref, g_ref, w_ref, cos_ref, sin_ref, z_ref):
    h = _rms(x_ref[...], g_ref[...]).astype(BF16)
    cos, sin = cos_ref[...], sin_ref[...]
    half = DK_R // 2
    for part, scale in ((0, 1.0), (1, DK_R ** -0.5)):
        for hh in range(H_R):
            c0 = part * H_R * DK_R + hh * DK_R
            z = _dot(h, w_ref[:, c0:c0 + DK_R])
            z1, z2 = z[:, :half], z[:, half:]
            z_ref[:, c0:c0 + half] = (z1 * cos - z2 * sin) * scale
            z_ref[:, c0 + half:c0 + DK_R] = (z2 * cos + z1 * sin) * scale
    c0 = 2 * H_R * DK_R
    for j in range(2 * H_R):
        z_ref[:, c0 + j * DV_R:c0 + (j + 1) * DV_R] = _dot(h, w_ref[:, c0 + j * DV_R:c0 + (j + 1) * DV_R])


def _norm_proj(body, x, gain, w, cos, sin, out_widths, tm):
    m, d = x.shape
    tm = min(tm, m)
    assert m % tm == 0 and cos.shape[0] % tm == 0
    period = cos.shape[0] // tm
    n = w.shape[1]
    outs = pl.pallas_call(
        body,
        grid=(m // tm,),
        in_specs=[
            pl.BlockSpec((tm, d), lambda i: (i, 0)),
            pl.BlockSpec((1, d), lambda i: (0, 0)),
            pl.BlockSpec((d, n), lambda i: (0, 0), pipeline_mode=pl.Buffered(1)),
            pl.BlockSpec((tm, LANES), lambda i: (i % period, 0)),
            pl.BlockSpec((tm, LANES), lambda i: (i % period, 0)),
        ],
        out_specs=[pl.BlockSpec((tm, wd), lambda i: (i, 0)) for wd in out_widths],
        out_shape=[jax.ShapeDtypeStruct((m, wd), F32) for wd in out_widths],
        compiler_params=_params(("parallel",)),
    )(x, gain.reshape(1, d), w, cos, sin)
    return outs


def _attn_pairs(blocks, mask, sinks):
    lane = lax.broadcasted_iota(jnp.int32, (BLOCK, LANES), 1)
    first = lane < HEAD_DIM
    owns = (first, jnp.logical_not(first))
    both = [(b, half) for b in range(len(blocks)) for half in range(len(owns))]
    kbs = [k.astype(BF16) for _, k, _ in blocks]
    vbs = [jnp.concatenate([v.astype(BF16), jnp.ones(v.shape, BF16)], axis=1) for _, _, v in blocks]
    s = [jnp.where(mask, _dot_nt(jnp.where(owns[half], blocks[b][0], 0.0).astype(BF16), kbs[b]), NEG_INF)
         for b, half in both]
    m = [jnp.max(x, axis=-1, keepdims=True) for x in s]
    if sinks is not None:
        m = [jnp.maximum(x, sinks[half]) for x, (_, half) in zip(m, both)]
    p = [jnp.exp(x - y).astype(BF16) for x, y in zip(s, m)]
    pv_l = [_dot(x, vbs[b]) for x, (b, _) in zip(p, both)]
    out = []
    for b in range(len(blocks)):
        i0, i1 = len(owns) * b, len(owns) * b + 1
        l0, l1 = pv_l[i0][:, LANES:], pv_l[i1][:, LANES:]
        if sinks is not None:
            l0 = l0 + jnp.exp(sinks[0] - m[i0])
            l1 = l1 + jnp.exp(sinks[1] - m[i1])
        out.append((jnp.where(first, m[i0], m[i1]), jnp.where(first, l0, l1),
                    jnp.where(first, pv_l[i0][:, :LANES], pv_l[i1][:, :LANES])))
    return out


def _prompt_attn_body(sink_ref, q0_ref, k0_ref, v0_ref, q1_ref, k1_ref, v1_ref, q2_ref, k2_ref, v2_ref,
                      qb_ref, kb_ref, vb_ref, oa_ref, ob_ref, acc_ref, m_ref, l_ref, *, seq):
    pair = pl.program_id(1)
    scale = HEAD_DIM ** -0.5
    row = lax.broadcasted_iota(jnp.int32, (BLOCK, 2 * BLOCK), 0)
    col = lax.broadcasted_iota(jnp.int32, (BLOCK, 2 * BLOCK), 1)
    causal = (lax.broadcasted_iota(jnp.int32, (BLOCK, BLOCK), 1)
              <= lax.broadcasted_iota(jnp.int32, (BLOCK, BLOCK), 0))

    def window_mask(min_back):
        return jnp.where(col < BLOCK, col - row - min_back, row - (col - BLOCK)) >= 0

    def rows(ref, start, n, stride):
        return ref[pl.ds(start, n, stride=stride), :] if stride > 1 else ref[pl.ds(start, n), :]

    def group_pass(q_ref, k_ref, v_ref, dil, min_back, sinks, emit):
        span = dil * BLOCK
        nblk = seq // span

        def run(starts, back, keys, mask):
            blocks = [(rows(q_ref, s, BLOCK, dil) * scale, rows(k_ref, s - back, keys, dil),
                       rows(v_ref, s - back, keys, dil)) for s in starts]
            for s, mlp in zip(starts, _attn_pairs(blocks, mask, sinks)):
                emit(s, mlp)

        for r0 in range(0, dil, BLOCKS_IN_FLIGHT):
            run(list(range(r0, min(r0 + BLOCKS_IN_FLIGHT, dil))), 0, BLOCK, causal)
        if nblk == 1:
            return
        mask = window_mask(min_back)
        per_iter = BLOCKS_IN_FLIGHT if (nblk - 1) % BLOCKS_IN_FLIGHT == 0 else 1
        trips = (nblk - 1) // per_iter

        def block_start(r, lb):
            return lb * span + r if isinstance(lb, int) else pl.multiple_of(lb * span, span) + r

        for r in range(dil):
            def body(it, carry, r=r):
                run([block_start(r, 1 + it * per_iter + j) for j in range(per_iter)], span, 2 * BLOCK, mask)
                return carry

            if trips == 1:
                body(0, 0)
            else:
                lax.fori_loop(0, trips, body, 0)

    def merge_emit(dil, last):
        def emit(start, mlp):
            m, l, pv = mlp
            idx = pl.ds(start, BLOCK, stride=dil)
            m_old = m_ref[idx, :]
            m_new = jnp.maximum(m_old, m)
            a = jnp.exp(m_old - m_new)
            b = jnp.exp(m - m_new)
            acc = a * acc_ref[idx, :] + b * pv
            den = a * l_ref[idx, :] + b * l
            if last:
                oa_ref[idx, :] = acc / den
            else:
                m_ref[idx, :] = m_new
                acc_ref[idx, :] = acc
                l_ref[idx, :] = den
        return emit

    def first_emit(start, mlp):
        idx = pl.ds(start, BLOCK)
        m_ref[idx, :], l_ref[idx, :], acc_ref[idx, :] = mlp

    groups = ((q0_ref, k0_ref, v0_ref), (q1_ref, k1_ref, v1_ref), (q2_ref, k2_ref, v2_ref))
    for g, (win, dil) in enumerate(DIL_PAIRS):
        emit = first_emit if g == 0 else merge_emit(dil, last=g == N_DIL - 1)
        group_pass(*groups[g], dil, BLOCK - win // dil, None, emit)

    sinks = [sink_ref[0, pair + G_B * half] for half in range(H_BKV)]

    def swa_emit(start, mlp):
        m, l, pv = mlp
        ob_ref[pl.ds(start, BLOCK), :] = pv / l

    group_pass(qb_ref, kb_ref, vb_ref, 1, BLOCK + 1 - SWA_WINDOW, sinks, swa_emit)


def _prompt_attn(qa, kvs, zb, sinks, nseq, seq):
    assert seq % (BLOCK * DIL_PAIRS[-1][1]) == 0 and KVB == LANES
    pairs = HW // LANES
    view = lambda a: a.reshape(nseq, seq, a.shape[1])
    col = lambda j: pl.BlockSpec((None, seq, LANES), lambda n, p: (n, 0, j(p)))
    in_specs = [pl.BlockSpec(memory_space=pltpu.SMEM)]
    args = [sinks.reshape(1, N_HEADS)]
    for g in range(N_DIL):
        in_specs += [col(lambda p, g=g: g * pairs + p), col(lambda p: p), col(lambda p: pairs + p)]
        args += [view(qa), view(kvs[g]), view(kvs[g])]
    in_specs += [col(lambda p: p), col(lambda p: pairs), col(lambda p: pairs + 1)]
    args += [view(zb)] * 3
    out_spec = pl.BlockSpec((None, seq, LANES), lambda n, p: (n, 0, p))
    oa, ob = pl.pallas_call(
        functools.partial(_prompt_attn_body, seq=seq),
        grid=(nseq, pairs),
        in_specs=in_specs,
        out_specs=[out_spec, out_spec],
        out_shape=[jax.ShapeDtypeStruct((nseq, seq, HW), F32)] * 2,
        scratch_shapes=[pltpu.VMEM((seq, LANES), F32)] * 3,
        compiler_params=_params(("parallel", "parallel")),
    )(*args)
    return oa.reshape(nseq * seq, HW), ob.reshape(nseq * seq, HW)


def _sample_attn_body(qa_ref, kn0_ref, kn1_ref, kn2_ref, zb_ref, c0_ref, c1_ref, c2_ref, cs_ref, sink_ref,
                      y_ref, pvn_ref, *, t_new):
    scale = HEAD_DIM ** -0.5
    r_i = lax.broadcasted_iota(jnp.int32, (N_HEADS * Q_PAD, HW), 0)
    l_i = lax.broadcasted_iota(jnp.int32, (N_HEADS * Q_PAD, HW), 1)
    own_head = (r_i >> Q_SHIFT) == (l_i >> HEAD_SHIFT)
    t_q = lax.broadcasted_iota(jnp.int32, (Q_PAD, 1), 0)
    new_key = lax.broadcasted_iota(jnp.int32, (Q_PAD, BLOCK), 1)

    def pad_rows(x, n):
        return jnp.concatenate([x, jnp.zeros((n - x.shape[0], x.shape[1]), x.dtype)], axis=0)

    def head_cols(ref, base, h):
        return ref[:, base + h * HEAD_DIM:base + (h + 1) * HEAD_DIM]

    kn_refs = (kn0_ref, kn1_ref, kn2_ref)
    c_refs = (c0_ref, c1_ref, c2_ref)
    stats = [[] for _ in range(N_HEADS)]
    for g, (win, dil) in enumerate(DIL_PAIRS):
        c_ref = c_refs[g]
        wb = c_ref.shape[-1]
        q_all = pad_rows(qa_ref[:, g * HW:(g + 1) * HW] * scale, Q_PAD)
        q_exp = jnp.where(own_head, jnp.concatenate([q_all] * N_HEADS, axis=0), 0.0).astype(BF16)
        s_new_all = _dot_nt(q_exp, pad_rows(kn_refs[g][:, :HW].astype(BF16), BLOCK))
        back = t_q - new_key
        mask_new = jnp.logical_and(jnp.logical_and(back >= 0, new_key < t_new), (back & (dil - 1)) == 0)
        ahead = lax.broadcasted_iota(jnp.int32, (Q_PAD, wb), 1) - t_q
        mask_cache = jnp.logical_and(ahead >= 0, (ahead & (dil - 1)) == 0)
        p_news = []
        partial = []
        for h in range(N_HEADS):
            qh = pad_rows(head_cols(qa_ref, g * HW, h) * scale, Q_PAD).astype(BF16)
            s_c = jnp.where(mask_cache, _dot(qh, c_ref[0, h].astype(BF16)), NEG_INF)
            s_n = jnp.where(mask_new, s_new_all[h * Q_PAD:(h + 1) * Q_PAD], NEG_INF)
            m = jnp.maximum(jnp.max(s_c, axis=-1, keepdims=True), jnp.max(s_n, axis=-1, keepdims=True))
            p_c = jnp.exp(s_c - m)
            p_n = jnp.exp(s_n - m)
            l = jnp.sum(p_c, axis=-1, keepdims=True) + jnp.sum(p_n, axis=-1, keepdims=True)
            partial.append((m, l, _dot_nt(p_c.astype(BF16), c_ref[1, h].astype(BF16))))
            p_news.append(p_n.astype(BF16))
        pvn_ref[...] = _dot(jnp.concatenate(p_news, axis=0), pad_rows(kn_refs[g][:, HW:].astype(BF16), BLOCK))
        for h, (m, l, pv) in enumerate(partial):
            pv = pv + pvn_ref[h * Q_PAD:(h + 1) * Q_PAD, h * HEAD_DIM:(h + 1) * HEAD_DIM]
            stats[h].append((m, l, pv))
    for h in range(N_HEADS):
        m_all = functools.reduce(jnp.maximum, [s[0] for s in stats[h]])
        acc = jnp.zeros((Q_PAD, HEAD_DIM), F32)
        den = jnp.zeros((Q_PAD, 1), F32)
        for m, l, pv in stats[h]:
            a = jnp.exp(m - m_all)
            acc = acc + a * pv
            den = den + a * l
        y_ref[:, h * HEAD_DIM:(h + 1) * HEAD_DIM] = (acc / den)[0:t_new]

    t_g = lax.broadcasted_iota(jnp.int32, (G_B * Q_PAD, 1), 0) & (Q_PAD - 1)
    key_g = lax.broadcasted_iota(jnp.int32, (G_B * Q_PAD, BLOCK), 1)
    mask_cache = key_g >= t_g + 1 + (BLOCK - SWA_WINDOW)
    mask_new = jnp.logical_and(t_g - key_g >= 0, key_g < t_new)
    for hk in range(H_BKV):
        qs = jnp.concatenate([pad_rows(head_cols(zb_ref, 0, _swa_slot(hk * G_B + gq)) * scale, Q_PAD)
                              for gq in range(G_B)], axis=0).astype(BF16)
        k_new = pad_rows(head_cols(zb_ref, HW, hk).astype(BF16), BLOCK)
        v_new = pad_rows(head_cols(zb_ref, HW + KVB, hk).astype(BF16), BLOCK)
        s_c = jnp.where(mask_cache, _dot(qs, cs_ref[0, hk].astype(BF16)), NEG_INF)
        s_n = jnp.where(mask_new, _dot_nt(qs, k_new), NEG_INF)
        sink = sink_ref[hk * G_B * Q_PAD:(hk + 1) * G_B * Q_PAD, 0:1]
        m = jnp.maximum(jnp.maximum(jnp.max(s_c, axis=-1, keepdims=True), jnp.max(s_n, axis=-1, keepdims=True)), sink)
        p_c = jnp.exp(s_c - m)
        p_n = jnp.exp(s_n - m)
        l = jnp.sum(p_c, axis=-1, keepdims=True) + jnp.sum(p_n, axis=-1, keepdims=True) + jnp.exp(sink - m)
        o = (_dot_nt(p_c.astype(BF16), cs_ref[1, hk].astype(BF16)) + _dot(p_n.astype(BF16), v_new)) / l
        for gq in range(G_B):
            slot = _swa_slot(hk * G_B + gq)
            y_ref[:, HW + slot * HEAD_DIM:HW + (slot + 1) * HEAD_DIM] = o[gq * Q_PAD:gq * Q_PAD + t_new]


def _sample_attn(qa, kns, zb, caches, cache_swa, sinks, e, nreq, t_new):
    assert t_new <= Q_PAD
    qv = qa.reshape(nreq, t_new, N_DIL * HW)
    knv = [k.reshape(nreq, t_new, 2 * HW) for k in kns]
    zbv = zb.reshape(nreq, t_new, HW + 2 * KVB)
    to_native = lambda c: jnp.transpose(c, (0, 1, 3, 4, 5, 2))
    cv = []
    c_specs = []
    for (win, dil), c in zip(DIL_PAIRS, caches):
        wb = c.shape[2]
        assert wb == win and wb // dil == BLOCK
        cv.append(to_native(c))
        c_specs.append(pl.BlockSpec((None, None, 2, N_HEADS, HEAD_DIM, wb), lambda b: (e, b, 0, 0, 0, 0)))
    assert cache_swa.shape[2] == BLOCK
    sink_rows = jnp.broadcast_to(jnp.repeat(sinks, Q_PAD)[:, None], (N_HEADS * Q_PAD, LANES))
    tok = lambda wd: pl.BlockSpec((None, t_new, wd), lambda b: (b, 0, 0))
    out = pl.pallas_call(
        functools.partial(_sample_attn_body, t_new=t_new),
        grid=(nreq,),
        in_specs=[tok(N_DIL * HW), tok(2 * HW), tok(2 * HW), tok(2 * HW), tok(HW + 2 * KVB)] + c_specs + [
            pl.BlockSpec((None, None, 2, H_BKV, HEAD_DIM, BLOCK), lambda b: (e, b, 0, 0, 0, 0)),
            pl.BlockSpec((N_HEADS * Q_PAD, LANES), lambda b: (0, 0)),
        ],
        out_specs=tok(2 * HW),
        out_shape=jax.ShapeDtypeStruct((nreq, t_new, 2 * HW), F32),
        scratch_shapes=[pltpu.VMEM((N_HEADS * Q_PAD, HW), F32)],
        compiler_params=_params(("parallel",)),
    )(qv, *knv, zbv, *cv, to_native(cache_swa), sink_rows)
    return out.reshape(nreq * t_new, 2 * HW)


def _out_proj_body(*refs):
    *y_refs, x_ref, g_ref, w_ref, out_ref = refs
    mo = None
    r0 = 0
    for y_ref in y_refs:
        part = _dot(y_ref[...].astype(BF16), w_ref[r0:r0 + y_ref.shape[1], :])
        mo = part if mo is None else mo + part
        r0 += y_ref.shape[1]
    out_ref[...] = x_ref[...] + _rms(mo, g_ref[...])


def _out_proj(ys, x, gain, w, tm):
    m, d = x.shape
    tm = min(tm, m)
    assert m % tm == 0 and sum(y.shape[1] for y in ys) == w.shape[0]
    row = lambda wd: pl.BlockSpec((tm, wd), lambda i: (i, 0))
    return pl.pallas_call(
        _out_proj_body,
        grid=(m // tm,),
        in_specs=[row(y.shape[1]) for y in ys] + [
            row(d),
            pl.BlockSpec((1, d), lambda i: (0, 0)),
            pl.BlockSpec(w.shape, lambda i: (0, 0), pipeline_mode=pl.Buffered(1)),
        ],
        out_specs=row(d),
        out_shape=jax.ShapeDtypeStruct((m, d), F32),
        compiler_params=_params(("parallel",)),
    )(*ys, x, gain.reshape(1, d), w)


def _mlp_body(x_ref, g_pre_ref, g_post_ref, w1_ref, w2_ref, out_ref, h_ref, acc_ref):
    j = pl.program_id(1)

    @pl.when(j == 0)
    def _():
        h_ref[...] = _rms(x_ref[...], g_pre_ref[...]).astype(BF16)
        acc_ref[...] = jnp.zeros_like(acc_ref)

    a = jnp.maximum(_dot(h_ref[...], w1_ref[...]), 0.0)
    acc_ref[...] += _dot((a * a).astype(BF16), w2_ref[...])

    @pl.when(j == pl.num_programs(1) - 1)
    def _():
        out_ref[...] = x_ref[...] + _rms(acc_ref[...], g_post_ref[...])


def _mlp(x, g_pre, g_post, w1, w2, tm, tf):
    m, d = x.shape
    ff = w1.shape[1]
    tm = min(tm, m)
    assert m % tm == 0 and ff % tf == 0
    return pl.pallas_call(
        _mlp_body,
        grid=(m // tm, ff // tf),
        in_specs=[
            pl.BlockSpec((tm, d), lambda i, j: (i, 0)),
            pl.BlockSpec((1, d), lambda i, j: (0, 0)),
            pl.BlockSpec((1, d), lambda i, j: (0, 0)),
            pl.BlockSpec((d, tf), lambda i, j: (0, j)),
            pl.BlockSpec((tf, d), lambda i, j: (j, 0)),
        ],
        out_specs=pl.BlockSpec((tm, d), lambda i, j: (i, 0)),
        out_shape=jax.ShapeDtypeStruct((m, d), F32),
        scratch_shapes=[pltpu.VMEM((tm, d), BF16), pltpu.VMEM((tm, d), F32)],
        compiler_params=_params(("parallel", "arbitrary")),
    )(x, g_pre.reshape(1, d), g_post.reshape(1, d), w1, w2)


def _log_gammas():
    return jnp.asarray(np.log(1.0 - 2.0 ** (-5.0 - np.arange(H_R, dtype=np.float64))), F32)


def _group_norm_gate(o, gate):
    mu = jnp.mean(o, axis=-1, keepdims=True)
    var = jnp.mean(jnp.square(o - mu), axis=-1, keepdims=True)
    o = (o - mu) * lax.rsqrt(var + GN_EPS)
    return gate / (1.0 + jnp.exp(-gate)) * o


def _ret_prompt_body(lg_ref, q_ref, k_ref, v_ref, gate_ref, y_ref, st_ref, s_ref):
    c = pl.program_id(1)

    @pl.when(c == 0)
    def _():
        s_ref[...] = jnp.zeros_like(s_ref)

    chunk = q_ref.shape[0]
    row = lax.broadcasted_iota(jnp.int32, (chunk, chunk), 0)
    col = lax.broadcasted_iota(jnp.int32, (chunk, chunk), 1)
    diff = (row - col).astype(F32)
    ii = lax.broadcasted_iota(jnp.int32, (chunk, 1), 0).astype(F32)
    for h in range(H_R):
        lg = lg_ref[h]
        decay = jnp.where(diff >= 0, jnp.exp(jnp.maximum(diff, 0.0) * lg), 0.0)
        q = q_ref[:, h * DK_R:(h + 1) * DK_R]
        k = k_ref[:, h * DK_R:(h + 1) * DK_R]
        v = v_ref[:, h * DV_R:(h + 1) * DV_R].astype(BF16)
        a = _dot_nt(q.astype(BF16), k.astype(BF16)) * decay
        s_prev = s_ref[h]
        o = _dot(a.astype(BF16), v) + _dot((q * jnp.exp((ii + 1.0) * lg)).astype(BF16), s_prev.astype(BF16))
        kd = (k * jnp.exp((chunk - 1.0 - ii) * lg)).astype(BF16)
        s_ref[h] = jnp.exp(jnp.full((1, 1), chunk, F32) * lg) * s_prev + _dot_tn(kd, v)
        y_ref[:, h * DV_R:(h + 1) * DV_R] = _group_norm_gate(o, gate_ref[:, h * DV_R:(h + 1) * DV_R])

    @pl.when(c == pl.num_programs(1) - 1)
    def _():
        st_ref[...] = s_ref[...]


def _ret_prompt(zr, nseq, seq):
    nc = seq // RET_CHUNK
    zv = zr.reshape(nseq, seq, zr.shape[1])
    qk_w, v_w = H_R * DK_R, H_R * DV_R
    assert v_w == 2 * qk_w
    y, st = pl.pallas_call(
        _ret_prompt_body,
        grid=(nseq, nc),
        in_specs=[
            pl.BlockSpec(memory_space=pltpu.SMEM),
            pl.BlockSpec((None, RET_CHUNK, qk_w), lambda n, c: (n, c, 0)),
            pl.BlockSpec((None, RET_CHUNK, qk_w), lambda n, c: (n, c, 1)),
            pl.BlockSpec((None, RET_CHUNK, v_w), lambda n, c: (n, c, 1)),
            pl.BlockSpec((None, RET_CHUNK, v_w), lambda n, c: (n, c, 2)),
        ],
        out_specs=[
            pl.BlockSpec((None, RET_CHUNK, v_w), lambda n, c: (n, c, 0)),
            pl.BlockSpec((None, H_R, DK_R, DV_R), lambda n, c: (n, 0, 0, 0)),
        ],
        out_shape=[
            jax.ShapeDtypeStruct((nseq, seq, v_w), F32),
            jax.ShapeDtypeStruct((nseq, H_R, DK_R, DV_R), F32),
        ],
        scratch_shapes=[pltpu.VMEM((H_R, DK_R, DV_R), F32)],
        compiler_params=_params(("parallel", "arbitrary")),
    )(_log_gammas(), zv, zv, zv, zv)
    return y.reshape(nseq * seq, v_w), st


def _ret_sample_body(*refs, t_new, has_prev):
    lg_ref, z_ref, s_in_ref = refs[:3]
    y_ref, s_out_ref, kpad_ref, vpad_ref = refs[4:] if has_prev else refs[3:]

    @pl.when(pl.program_id(0) == 0)
    def _():
        kpad_ref[...] = jnp.zeros_like(kpad_ref)
        vpad_ref[...] = jnp.zeros_like(vpad_ref)

    rows = 8
    row = lax.broadcasted_iota(jnp.int32, (rows, BLOCK), 0)
    col = lax.broadcasted_iota(jnp.int32, (rows, BLOCK), 1)
    diff = (row - col).astype(F32)
    ii = lax.broadcasted_iota(jnp.int32, (rows, 1), 0).astype(F32)
    ik = lax.broadcasted_iota(jnp.int32, (t_new, 1), 0).astype(F32)
    zq = jnp.zeros((rows - t_new, DK_R), F32)
    for h in range(H_R):
        lg = lg_ref[h]
        q = jnp.concatenate([z_ref[:, h * DK_R:(h + 1) * DK_R], zq], axis=0)
        k = z_ref[:, (H_R + h) * DK_R:(H_R + h + 1) * DK_R]
        v = z_ref[:, 2 * H_R * DK_R + h * DV_R:2 * H_R * DK_R + (h + 1) * DV_R]
        gate = z_ref[:, 2 * H_R * DK_R + (H_R + h) * DV_R:2 * H_R * DK_R + (H_R + h + 1) * DV_R]
        kpad_ref[0:t_new, :] = k
        vpad_ref[0:t_new, :] = v
        decay = jnp.where(diff >= 0, jnp.exp(jnp.maximum(diff, 0.0) * lg), 0.0)
        a = _dot_nt(q.astype(BF16), kpad_ref[...].astype(BF16)) * decay
        s_prev = s_in_ref[h]
        vb = vpad_ref[...].astype(BF16)
        o = _dot(a.astype(BF16), vb) + _dot((q * jnp.exp((ii + 1.0) * lg)).astype(BF16), s_prev.astype(BF16))
        kpad_ref[0:t_new, :] = k * jnp.exp((t_new - 1.0 - ik) * lg)
        upd = _dot_tn(kpad_ref[...].astype(BF16), vb)
        s_out_ref[h] = jnp.exp(jnp.full((1, 1), t_new, F32) * lg) * s_prev + upd
        y_ref[:, h * DV_R:(h + 1) * DV_R] = _group_norm_gate(o[0:t_new], gate)


def _ret_sample(zr, state, new_state, o, nreq, t_new):
    zv = zr.reshape(nreq, t_new, zr.shape[1])
    has_prev = new_state is not None
    state_blk = pl.BlockSpec((None, None, H_R, DK_R, DV_R), lambda b: (o, b, 0, 0, 0))
    in_specs = [
        pl.BlockSpec(memory_space=pltpu.SMEM),
        pl.BlockSpec((None, t_new, zr.shape[1]), lambda b: (b, 0, 0)),
        state_blk,
    ]
    args = [_log_gammas(), zv, state]
    if has_prev:
        in_specs.append(pl.BlockSpec(memory_space=pl.ANY))
        args.append(new_state)
    y, st = pl.pallas_call(
        functools.partial(_ret_sample_body, t_new=t_new, has_prev=has_prev),
        grid=(nreq,),
        in_specs=in_specs,
        out_specs=[pl.BlockSpec((None, t_new, H_R * DV_R), lambda b: (b, 0, 0)), state_blk],
        out_shape=[
            jax.ShapeDtypeStruct((nreq, t_new, H_R * DV_R), F32),
            jax.ShapeDtypeStruct(state.shape, F32),
        ],
        input_output_aliases={3: 1} if has_prev else {},
        scratch_shapes=[pltpu.VMEM((BLOCK, DK_R), F32), pltpu.VMEM((BLOCK, DV_R), F32)],
        compiler_params=_params(("arbitrary",)),
    )(*args)
    return y.reshape(nreq * t_new, H_R * DV_R), st


def kernel(x_prompt, x_sample, cache_dil_w128, cache_dil_w512, cache_dil_w2048, cache_swa, state_ret,
           norm_gains, w_in_mix, w_out_mix, attn_sinks, w_in_ret, w_out_ret, w_ff1, w_ff2):
    nseq, seq, d = x_prompt.shape
    nreq, t_new, _ = x_sample.shape
    depth = norm_gains.shape[0]
    assert seq % (BLOCK * DIL_PAIRS[-1][1]) == 0 and t_new <= DIL_PAIRS[1][1]
    xp = x_prompt.reshape(nseq * seq, d)
    xs = x_sample.reshape(nreq * t_new, d)
    dil_caches = (cache_dil_w128, cache_dil_w512, cache_dil_w2048)
    pos_p = np.arange(seq)
    pos_s = PAST_LEN + np.arange(t_new)
    rows_s = min(nreq * t_new, 512)
    rope_p = {dh: _rope_tables(pos_p, dh) for dh in (HEAD_DIM, DK_R)}
    rope_s = {dh: _rope_tables(np.tile(pos_s, rows_s // t_new), dh) for dh in (HEAD_DIM, DK_R)}
    mix_widths = (N_DIL * HW, 2 * HW, 2 * HW, 2 * HW, HW + 2 * KVB)

    dil_p = [[] for _ in DIL_PAIRS]
    dil_s = [[] for _ in DIL_PAIRS]
    swa_p, swa_s, ret_p = [], [], []
    ret_s = None
    for layer in range(depth):
        gn = norm_gains[layer]
        if layer % 2 == 0:
            e = layer // 2
            w_in = _slot_order(w_in_mix[e], 1, N_DIL * 3 * HW).astype(BF16)
            w_out = _slot_order(w_out_mix[e], 0, HW).astype(BF16)
            qa_p, *kv_p, zb_p = _norm_proj(_mix_proj_body, xp, gn[0], w_in, *rope_p[HEAD_DIM], mix_widths, 256)
            qa_s, *kv_s, zb_s = _norm_proj(_mix_proj_body, xs, gn[0], w_in, *rope_s[HEAD_DIM], mix_widths, rows_s)
            oa_p, ob_p = _prompt_attn(qa_p, kv_p, zb_p, attn_sinks[e], nseq, seq)
            xp = _out_proj([oa_p, ob_p], xp, gn[1], w_out, 512)
            y_s = _sample_attn(qa_s, kv_s, zb_s, dil_caches, cache_swa, attn_sinks[e], e, nreq, t_new)
            xs = _out_proj([y_s], xs, gn[1], w_out, 512)
            for g, (win, _) in enumerate(DIL_PAIRS):
                wb = min(win, seq)
                dil_p[g].append(kv_p[g].reshape(nseq, seq, 2, N_HEADS, HEAD_DIM)[:, seq - wb:])
                dil_s[g].append(kv_s[g].reshape(nreq, t_new, 2, N_HEADS, HEAD_DIM))
            wb = min(SWA_WINDOW, seq)
            swa_p.append(zb_p[:, HW:].reshape(nseq, seq, 2, H_BKV, HEAD_DIM)[:, seq - wb:])
            swa_s.append(zb_s[:, HW:].reshape(nreq, t_new, 2, H_BKV, HEAD_DIM))
        else:
            o = layer // 2
            w_in = w_in_ret[o].astype(BF16)
            w_out = w_out_ret[o].astype(BF16)
            width = (w_in.shape[1],)
            zr_p, = _norm_proj(_ret_proj_body, xp, gn[0], w_in, *rope_p[DK_R], width, 256)
            zr_s, = _norm_proj(_ret_proj_body, xs, gn[0], w_in, *rope_s[DK_R], width, rows_s)
            y_p, st_p = _ret_prompt(zr_p, nseq, seq)
            y_s, ret_s = _ret_sample(zr_s, state_ret, ret_s, o, nreq, t_new)
            xp = _out_proj([y_p], xp, gn[1], w_out, 512)
            xs = _out_proj([y_s], xs, gn[1], w_out, 512)
            ret_p.append(st_p)
        w1 = w_ff1[layer].astype(BF16)
        w2 = w_ff2[layer].astype(BF16)
        xp = _mlp(xp, gn[2], gn[3], w1, w2, 1024, 512)
        xs = _mlp(xs, gn[2], gn[3], w1, w2, 512, 512)
    return (xp.reshape(nseq, seq, d), xs.reshape(nreq, t_new, d),
            jnp.stack(dil_p[0]), jnp.stack(dil_s[0]),
            jnp.stack(dil_p[1]), jnp.stack(dil_s[1]),
            jnp.stack(dil_p[2]), jnp.stack(dil_s[2]),
            jnp.stack(swa_p), jnp.stack(swa_s),
            jnp.stack(ret_p), ret_s)
```

```python
import functools

import numpy as np
import jax
import jax.numpy as jnp
from jax import lax
from jax.experimental import pallas as pl
from jax.experimental.pallas import tpu as pltpu

F32 = jnp.float32
BF16 = jnp.bfloat16

HEAD_DIM = 64
ROPE_HALF = HEAD_DIM // 2
N_HEADS = 8
HW = N_HEADS * HEAD_DIM
DIL_PAIRS = ((128, 1), (512, 4), (2048, 16))
N_DIL = len(DIL_PAIRS)
H_BKV = 2
G_B = N_HEADS // H_BKV
KVB = H_BKV * HEAD_DIM
SWA_WINDOW = 128
H_R = 4
DK_R = 256
DV_R = 512
RET_CHUNK = 128
BLOCK = 128
PAST_LEN = 8192
ROPE_THETA = 10000.0
NORM_EPS = 1e-6
GN_EPS = 1e-5
NEG_INF = -1e30
LANES = 128
BLOCKS_IN_FLIGHT = 3
Q_PAD = 8
Q_SHIFT = Q_PAD.bit_length() - 1
HEAD_SHIFT = HEAD_DIM.bit_length() - 1
VMEM_LIMIT = 56 * 1024 * 1024


def _swa_slot(hq):
    return (hq % G_B) * H_BKV + hq // G_B


def _slot_order(w, axis, base):
    piece = lambda lo, hi: lax.slice_in_dim(w, lo, hi, axis=axis)
    q = piece(base, base + HW)
    shape = q.shape[:axis] + (H_BKV, G_B, HEAD_DIM) + q.shape[axis + 1:]
    q = jnp.swapaxes(q.reshape(shape), axis, axis + 1).reshape(q.shape)
    return jnp.concatenate([piece(0, base), q, piece(base + HW, w.shape[axis])], axis=axis)


def _params(sem):
    return pltpu.CompilerParams(dimension_semantics=sem, vmem_limit_bytes=VMEM_LIMIT)


def _rms(x, g):
    return x * lax.rsqrt(jnp.mean(x * x, axis=-1, keepdims=True) + NORM_EPS) * g


def _dot(a, b):
    return jnp.dot(a, b, preferred_element_type=F32)


def _dot_nt(a, b):
    return lax.dot_general(a, b, (((1,), (1,)), ((), ())), preferred_element_type=F32)


def _dot_tn(a, b):
    return lax.dot_general(a, b, (((0,), (0,)), ((), ())), preferred_element_type=F32)


def _rope_tables(pos, dh):
    half = dh // 2
    inv = ROPE_THETA ** (-np.arange(half, dtype=np.float64) * (2.0 / dh))
    ang = np.asarray(pos, np.float64)[:, None] * inv[None, :]
    c, s = np.cos(ang), np.sin(ang)
    if dh == HEAD_DIM:
        cos = np.tile(np.concatenate([c, c], axis=1), (1, LANES // dh))
        sin = np.tile(np.concatenate([-s, s], axis=1), (1, LANES // dh))
    else:
        cos, sin = c, s
    return jnp.asarray(cos, F32), jnp.asarray(sin, F32)


def _rope_tables_t(pos):
    inv = ROPE_THETA ** (-np.arange(ROPE_HALF, dtype=np.float64) * (2.0 / HEAD_DIM))
    ang = inv[:, None] * np.asarray(pos, np.float64)[None, :]
    return jnp.asarray(np.cos(ang), F32), jnp.asarray(np.sin(ang), F32)


def _cache_rows_body(*refs, rope_heads):
    x_ref, g_ref, wt_ref, cos_ref, sin_ref = refs[:5]
    out_ref = refs[-1]
    h = _rms(x_ref[...], g_ref[...]).astype(BF16)
    zt = _dot_nt(wt_ref[...], h)
    cos, sin = cos_ref[...], sin_ref[...]
    for hh in range(rope_heads):
        r0 = hh * HEAD_DIM
        k1, k2 = zt[r0:r0 + ROPE_HALF], zt[r0 + ROPE_HALF:r0 + HEAD_DIM]
        out_ref[r0:r0 + ROPE_HALF, :] = k1 * cos - k2 * sin
        out_ref[r0 + ROPE_HALF:r0 + HEAD_DIM, :] = k2 * cos + k1 * sin
    out_ref[rope_heads * HEAD_DIM:, :] = zt[rope_heads * HEAD_DIM:]


def _cache_rows(x, gain, wt, cos_t, sin_t, buf, e, n_layers, nseq, seq, wb, tm):
    d = x.shape[1]
    cols = wt.shape[0]
    tm = min(tm, wb)
    assert wb % tm == 0 and seq % tm == 0 and (seq - wb) % tm == 0
    tiles, first = seq // tm, (seq - wb) // tm
    in_specs = [
        pl.BlockSpec((tm, d), lambda n, j: (n * tiles + first + j, 0)),
        pl.BlockSpec((1, d), lambda n, j: (0, 0)),
        pl.BlockSpec((cols, d), lambda n, j: (0, 0)),
        pl.BlockSpec((ROPE_HALF, tm), lambda n, j: (0, first + j)),
        pl.BlockSpec((ROPE_HALF, tm), lambda n, j: (0, first + j)),
    ]
    args = [x, gain.reshape(1, d), wt, cos_t, sin_t]
    if buf is not None:
        in_specs.append(pl.BlockSpec(memory_space=pl.ANY))
        args.append(buf)
    return pl.pallas_call(
        functools.partial(_cache_rows_body, rope_heads=cols // (2 * HEAD_DIM)),
        grid=(nseq, wb // tm),
        in_specs=in_specs,
        out_specs=pl.BlockSpec((None, None, cols, tm), lambda n, j: (e, n, 0, j)),
        out_shape=jax.ShapeDtypeStruct((n_layers, nseq, cols, wb), F32),
        input_output_aliases={5: 0} if buf is not None else {},
        compiler_params=_params(("parallel", "parallel")),
    )(*args)


def _rope64(z, cos, sin):
    w = z.shape[1]
    reps = w // LANES
    lane = lax.broadcasted_iota(jnp.int32, z.shape, 1)
    first_half = (lane & (HEAD_DIM - 1)) < ROPE_HALF
    partner = jnp.where(first_half, pltpu.roll(z, w - ROPE_HALF, 1), pltpu.roll(z, ROPE_HALF, 1))
    if reps > 1:
        cos = jnp.concatenate([cos] * reps, axis=1)
        sin = jnp.concatenate([sin] * reps, axis=1)
    return z * cos + partner * sin


def _mix_proj_body(x_ref, g_ref, w_ref, cos_ref, sin_ref, qa_ref, kv0_ref, kv1_ref, kv2_ref, zb_ref):
    h = _rms(x_ref[...], g_ref[...]).astype(BF16)
    cos, sin = cos_ref[...], sin_ref[...]
    kv_refs = (kv0_ref, kv1_ref, kv2_ref)
    for g in range(N_DIL):
        base = g * 3 * HW
        qa_ref[:, g * HW:(g + 1) * HW] = _rope64(_dot(h, w_ref[:, base:base + HW]), cos, sin)
        kv_refs[g][:, :HW] = _rope64(_dot(h, w_ref[:, base + HW:base + 2 * HW]), cos, sin)
        kv_refs[g][:, HW:] = _dot(h, w_ref[:, base + 2 * HW:base + 3 * HW])
    base = N_DIL * 3 * HW
    zb_ref[:, :HW] = _rope64(_dot(h, w_ref[:, base:base + HW]), cos, sin)
    zb_ref[:, HW:HW + KVB] = _rope64(_dot(h, w_ref[:, base + HW:base + HW + KVB]), cos, sin)
    zb_ref[:, HW + KVB:] = _dot(h, w_ref[:, base + HW + KVB:base + HW + 2 * KVB])


def _ret_proj_body(x_ref, g_ref, w_ref, cos_ref, sin_ref, z_ref):
    h = _rms(x_ref[...], g_ref[...]).astype(BF16)
    cos, sin = cos_ref[...], sin_ref[...]
    half = DK_R // 2
    for part, scale in ((0, 1.0), (1, DK_R ** -0.5)):
        for hh in range(H_R):
            c0 = part * H_R * DK_R + hh * DK_R
            z = _dot(h, w_ref[:, c0:c0 + DK_R])
            z1, z2 = z[:, :half], z[:, half:]
            z_ref[:, c0:c0 + half] = (z1 * cos - z2 * sin) * scale
            z_ref[:, c0 + half:c0 + DK_R] = (z2 * cos + z1 * sin) * scale
    c0 = 2 * H_R * DK_R
    for j in range(2 * H_R):
        z_ref[:, c0 + j * DV_R:c0 + (j + 1) * DV_R] = _dot(h, w_ref[:, c0 + j * DV_R:c0 + (j + 1) * DV_R])


def _norm_proj(body, x, gain, w, cos, sin, out_widths, tm):
    m, d = x.shape
    tm = min(tm, m)
    assert m % tm == 0 and cos.shape[0] % tm == 0
    period = cos.shape[0] // tm
    n = w.shape[1]
    outs = pl.pallas_call(
        body,
        grid=(m // tm,),
        in_specs=[
            pl.BlockSpec((tm, d), lambda i: (i, 0)),
            pl.BlockSpec((1, d), lambda i: (0, 0)),
            pl.BlockSpec((d, n), lambda i: (0, 0), pipeline_mode=pl.Buffered(1)),
            pl.BlockSpec((tm, LANES), lambda i: (i % period, 0)),
            pl.BlockSpec((tm, LANES), lambda i: (i % period, 0)),
        ],
        out_specs=[pl.BlockSpec((tm, wd), lambda i: (i, 0)) for wd in out_widths],
        out_shape=[jax.ShapeDtypeStruct((m, wd), F32) for wd in out_widths],
        compiler_params=_params(("parallel",)),
    )(x, gain.reshape(1, d), w, cos, sin)
    return outs


def _attn_pairs(blocks, mask, sinks):
    lane = lax.broadcasted_iota(jnp.int32, (BLOCK, LANES), 1)
    first = lane < HEAD_DIM
    owns = (first, jnp.logical_not(first))
    both = [(b, half) for b in range(len(blocks)) for half in range(len(owns))]
    kbs = [k.astype(BF16) for _, k, _ in blocks]
    vbs = [jnp.concatenate([v.astype(BF16), jnp.ones(v.shape, BF16)], axis=1) for _, _, v in blocks]
    s = [jnp.where(mask, _dot_nt(jnp.where(owns[half], blocks[b][0], 0.0).astype(BF16), kbs[b]), NEG_INF)
         for b, half in both]
    m = [jnp.max(x, axis=-1, keepdims=True) for x in s]
    if sinks is not None:
        m = [jnp.maximum(x, sinks[half]) for x, (_, half) in zip(m, both)]
    p = [jnp.exp(x - y).astype(BF16) for x, y in zip(s, m)]
    pv_l = [_dot(x, vbs[b]) for x, (b, _) in zip(p, both)]
    out = []
    for b in range(len(blocks)):
        i0, i1 = len(owns) * b, len(owns) * b + 1
        l0, l1 = pv_l[i0][:, LANES:], pv_l[i1][:, LANES:]
        if sinks is not None:
            l0 = l0 + jnp.exp(sinks[0] - m[i0])
            l1 = l1 + jnp.exp(sinks[1] - m[i1])
        out.append((jnp.where(first, m[i0], m[i1]), jnp.where(first, l0, l1),
                    jnp.where(first, pv_l[i0][:, :LANES], pv_l[i1][:, :LANES])))
    return out


def _prompt_attn_body(sink_ref, q0_ref, k0_ref, v0_ref, q1_ref, k1_ref, v1_ref, q2_ref, k2_ref, v2_ref,
                      qb_ref, kb_ref, vb_ref, oa_ref, ob_ref, acc_ref, m_ref, l_ref, *, seq):
    pair = pl.program_id(1)
    scale = HEAD_DIM ** -0.5
    row = lax.broadcasted_iota(jnp.int32, (BLOCK, 2 * BLOCK), 0)
    col = lax.broadcasted_iota(jnp.int32, (BLOCK, 2 * BLOCK), 1)
    causal = (lax.broadcasted_iota(jnp.int32, (BLOCK, BLOCK), 1)
              <= lax.broadcasted_iota(jnp.int32, (BLOCK, BLOCK), 0))

    def window_mask(min_back):
        return jnp.where(col < BLOCK, col - row - min_back, row - (col - BLOCK)) >= 0

    def rows(ref, start, n, stride):
        return ref[pl.ds(start, n, stride=stride), :] if stride > 1 else ref[pl.ds(start, n), :]

    def group_pass(q_ref, k_ref, v_ref, dil, min_back, sinks, emit):
        span = dil * BLOCK
        nblk = seq // span

        def run(starts, back, keys, mask):
            blocks = [(rows(q_ref, s, BLOCK, dil) * scale, rows(k_ref, s - back, keys, dil),
                       rows(v_ref, s - back, keys, dil)) for s in starts]
            for s, mlp in zip(starts, _attn_pairs(blocks, mask, sinks)):
                emit(s, mlp)

        for r0 in range(0, dil, BLOCKS_IN_FLIGHT):
            run(list(range(r0, min(r0 + BLOCKS_IN_FLIGHT, dil))), 0, BLOCK, causal)
        if nblk == 1:
            return
        mask = window_mask(min_back)
        per_iter = BLOCKS_IN_FLIGHT if (nblk - 1) % BLOCKS_IN_FLIGHT == 0 else 1
        trips = (nblk - 1) // per_iter

        def block_start(r, lb):
            return lb * span + r if isinstance(lb, int) else pl.multiple_of(lb * span, span) + r

        for r in range(dil):
            def body(it, carry, r=r):
                run([block_start(r, 1 + it * per_iter + j) for j in range(per_iter)], span, 2 * BLOCK, mask)
                return carry

            if trips == 1:
                body(0, 0)
            else:
                lax.fori_loop(0, trips, body, 0)

    def merge_emit(dil, last):
        def emit(start, mlp):
            m, l, pv = mlp
            idx = pl.ds(start, BLOCK, stride=dil)
            m_old = m_ref[idx, :]
            m_new = jnp.maximum(m_old, m)
            a = jnp.exp(m_old - m_new)
            b = jnp.exp(m - m_new)
            acc = a * acc_ref[idx, :] + b * pv
            den = a * l_ref[idx, :] + b * l
            if last:
                oa_ref[idx, :] = acc / den
            else:
                m_ref[idx, :] = m_new
                acc_ref[idx, :] = acc
                l_ref[idx, :] = den
        return emit

    def first_emit(start, mlp):
        idx = pl.ds(start, BLOCK)
        m_ref[idx, :], l_ref[idx, :], acc_ref[idx, :] = mlp

    groups = ((q0_ref, k0_ref, v0_ref), (q1_ref, k1_ref, v1_ref), (q2_ref, k2_ref, v2_ref))
    for g, (win, dil) in enumerate(DIL_PAIRS):
        emit = first_emit if g == 0 else merge_emit(dil, last=g == N_DIL - 1)
        group_pass(*groups[g], dil, BLOCK - win // dil, None, emit)

    sinks = [sink_ref[0, pair + G_B * half] for half in range(H_BKV)]

    def swa_emit(start, mlp):
        m, l, pv = mlp
        ob_ref[pl.ds(start, BLOCK), :] = pv / l

    group_pass(qb_ref, kb_ref, vb_ref, 1, BLOCK + 1 - SWA_WINDOW, sinks, swa_emit)


def _prompt_attn(qa, kvs, zb, sinks, nseq, seq):
    assert seq % (BLOCK * DIL_PAIRS[-1][1]) == 0 and KVB == LANES
    pairs = HW // LANES
    view = lambda a: a.reshape(nseq, seq, a.shape[1])
    col = lambda j: pl.BlockSpec((None, seq, LANES), lambda n, p: (n, 0, j(p)))
    in_specs = [pl.BlockSpec(memory_space=pltpu.SMEM)]
    args = [sinks.reshape(1, N_HEADS)]
    for g in range(N_DIL):
        in_specs += [col(lambda p, g=g: g * pairs + p), col(lambda p: p), col(lambda p: pairs + p)]
        args += [view(qa), view(kvs[g]), view(kvs[g])]
    in_specs += [col(lambda p: p), col(lambda p: pairs), col(lambda p: pairs + 1)]
    args += [view(zb)] * 3
    out_spec = pl.BlockSpec((None, seq, LANES), lambda n, p: (n, 0, p))
    oa, ob = pl.pallas_call(
        functools.partial(_prompt_attn_body, seq=seq),
        grid=(nseq, pairs),
        in_specs=in_specs,
        out_specs=[out_spec, out_spec],
        out_shape=[jax.ShapeDtypeStruct((nseq, seq, HW), F32)] * 2,
        scratch_shapes=[pltpu.VMEM((seq, LANES), F32)] * 3,
        compiler_params=_params(("parallel", "parallel")),
    )(*args)
    return oa.reshape(nseq * seq, HW), ob.reshape(nseq * seq, HW)


def _sample_attn_body(qa_ref, kn0_ref, kn1_ref, kn2_ref, zb_ref, c0_ref, c1_ref, c2_ref, cs_ref, sink_ref,
                      y_ref, pvn_ref, *, t_new):
    scale = HEAD_DIM ** -0.5
    r_i = lax.broadcasted_iota(jnp.int32, (N_HEADS * Q_PAD, HW), 0)
    l_i = lax.broadcasted_iota(jnp.int32, (N_HEADS * Q_PAD, HW), 1)
    own_head = (r_i >> Q_SHIFT) == (l_i >> HEAD_SHIFT)
    t_q = lax.broadcasted_iota(jnp.int32, (Q_PAD, 1), 0)
    new_key = lax.broadcasted_iota(jnp.int32, (Q_PAD, BLOCK), 1)

    def pad_rows(x, n):
        return jnp.concatenate([x, jnp.zeros((n - x.shape[0], x.shape[1]), x.dtype)], axis=0)

    def head_cols(ref, base, h):
        return ref[:, base + h * HEAD_DIM:base + (h + 1) * HEAD_DIM]

    kn_refs = (kn0_ref, kn1_ref, kn2_ref)
    c_refs = (c0_ref, c1_ref, c2_ref)
    stats = [[] for _ in range(N_HEADS)]
    for g, (win, dil) in enumerate(DIL_PAIRS):
        c_ref = c_refs[g]
        wb = c_ref.shape[-1]
        q_all = pad_rows(qa_ref[:, g * HW:(g + 1) * HW] * scale, Q_PAD)
        q_exp = jnp.where(own_head, jnp.concatenate([q_all] * N_HEADS, axis=0), 0.0).astype(BF16)
        s_new_all = _dot_nt(q_exp, pad_rows(kn_refs[g][:, :HW].astype(BF16), BLOCK))
        back = t_q - new_key
        mask_new = jnp.logical_and(jnp.logical_and(back >= 0, new_key < t_new), (back & (dil - 1)) == 0)
        ahead = lax.broadcasted_iota(jnp.int32, (Q_PAD, wb), 1) - t_q
        mask_cache = jnp.logical_and(ahead >= 0, (ahead & (dil - 1)) == 0)
        p_news = []
        partial = []
        for h in range(N_HEADS):
            qh = pad_rows(head_cols(qa_ref, g * HW, h) * scale, Q_PAD).astype(BF16)
            s_c = jnp.where(mask_cache, _dot(qh, c_ref[0, h].astype(BF16)), NEG_INF)
            s_n = jnp.where(mask_new, s_new_all[h * Q_PAD:(h + 1) * Q_PAD], NEG_INF)
            m = jnp.maximum(jnp.max(s_c, axis=-1, keepdims=True), jnp.max(s_n, axis=-1, keepdims=True))
            p_c = jnp.exp(s_c - m)
            p_n = jnp.exp(s_n - m)
            l = jnp.sum(p_c, axis=-1, keepdims=True) + jnp.sum(p_n, axis=-1, keepdims=True)
            partial.append((m, l, _dot_nt(p_c.astype(BF16), c_ref[1, h].astype(BF16))))
            p_news.append(p_n.astype(BF16))
        pvn_ref[...] = _dot(jnp.concatenate(p_news, axis=0), pad_rows(kn_refs[g][:, HW:].astype(BF16), BLOCK))
        for h, (m, l, pv) in enumerate(partial):
            pv = pv + pvn_ref[h * Q_PAD:(h + 1) * Q_PAD, h * HEAD_DIM:(h + 1) * HEAD_DIM]
            stats[h].append((m, l, pv))
    for h in range(N_HEADS):
        m_all = functools.reduce(jnp.maximum, [s[0] for s in stats[h]])
        acc = jnp.zeros((Q_PAD, HEAD_DIM), F32)
        den = jnp.zeros((Q_PAD, 1), F32)
        for m, l, pv in stats[h]:
            a = jnp.exp(m - m_all)
            acc = acc + a * pv
            den = den + a * l
        y_ref[:, h * HEAD_DIM:(h + 1) * HEAD_DIM] = (acc / den)[0:t_new]

    t_g = lax.broadcasted_iota(jnp.int32, (G_B * Q_PAD, 1), 0) & (Q_PAD - 1)
    key_g = lax.broadcasted_iota(jnp.int32, (G_B * Q_PAD, BLOCK), 1)
    mask_cache = key_g >= t_g + 1 + (BLOCK - SWA_WINDOW)
    mask_new = jnp.logical_and(t_g - key_g >= 0, key_g < t_new)
    for hk in range(H_BKV):
        qs = jnp.concatenate([pad_rows(head_cols(zb_ref, 0, _swa_slot(hk * G_B + gq)) * scale, Q_PAD)
                              for gq in range(G_B)], axis=0).astype(BF16)
        k_new = pad_rows(head_cols(zb_ref, HW, hk).astype(BF16), BLOCK)
        v_new = pad_rows(head_cols(zb_ref, HW + KVB, hk).astype(BF16), BLOCK)
        s_c = jnp.where(mask_cache, _dot(qs, cs_ref[0, hk].astype(BF16)), NEG_INF)
        s_n = jnp.where(mask_new, _dot_nt(qs, k_new), NEG_INF)
        sink = sink_ref[hk * G_B * Q_PAD:(hk + 1) * G_B * Q_PAD, 0:1]
        m = jnp.maximum(jnp.maximum(jnp.max(s_c, axis=-1, keepdims=True), jnp.max(s_n, axis=-1, keepdims=True)), sink)
        p_c = jnp.exp(s_c - m)
        p_n = jnp.exp(s_n - m)
        l = jnp.sum(p_c, axis=-1, keepdims=True) + jnp.sum(p_n, axis=-1, keepdims=True) + jnp.exp(sink - m)
        o = (_dot_nt(p_c.astype(BF16), cs_ref[1, hk].astype(BF16)) + _dot(p_n.astype(BF16), v_new)) / l
        for gq in range(G_B):
            slot = _swa_slot(hk * G_B + gq)
            y_ref[:, HW + slot * HEAD_DIM:HW + (slot + 1) * HEAD_DIM] = o[gq * Q_PAD:gq * Q_PAD + t_new]


def _sample_attn(qa, kns, zb, caches, cache_swa, sinks, e, nreq, t_new):
    assert t_new <= Q_PAD
    qv = qa.reshape(nreq, t_new, N_DIL * HW)
    knv = [k.reshape(nreq, t_new, 2 * HW) for k in kns]
    zbv = zb.reshape(nreq, t_new, HW + 2 * KVB)
    to_native = lambda c: jnp.transpose(c, (0, 1, 3, 4, 5, 2))
    cv = []
    c_specs = []
    for (win, dil), c in zip(DIL_PAIRS, caches):
        wb = c.shape[2]
        assert wb == win and wb // dil == BLOCK
        cv.append(to_native(c))
        c_specs.append(pl.BlockSpec((None, None, 2, N_HEADS, HEAD_DIM, wb), lambda b: (e, b, 0, 0, 0, 0)))
    assert cache_swa.shape[2] == BLOCK
    sink_rows = jnp.broadcast_to(jnp.repeat(sinks, Q_PAD)[:, None], (N_HEADS * Q_PAD, LANES))
    tok = lambda wd: pl.BlockSpec((None, t_new, wd), lambda b: (b, 0, 0))
    out = pl.pallas_call(
        functools.partial(_sample_attn_body, t_new=t_new),
        grid=(nreq,),
        in_specs=[tok(N_DIL * HW), tok(2 * HW), tok(2 * HW), tok(2 * HW), tok(HW + 2 * KVB)] + c_specs + [
            pl.BlockSpec((None, None, 2, H_BKV, HEAD_DIM, BLOCK), lambda b: (e, b, 0, 0, 0, 0)),
            pl.BlockSpec((N_HEADS * Q_PAD, LANES), lambda b: (0, 0)),
        ],
        out_specs=tok(2 * HW),
        out_shape=jax.ShapeDtypeStruct((nreq, t_new, 2 * HW), F32),
        scratch_shapes=[pltpu.VMEM((N_HEADS * Q_PAD, HW), F32)],
        compiler_params=_params(("parallel",)),
    )(qv, *knv, zbv, *cv, to_native(cache_swa), sink_rows)
    return out.reshape(nreq * t_new, 2 * HW)


def _out_proj_body(*refs):
    *y_refs, x_ref, g_ref, w_ref, out_ref = refs
    mo = None
    r0 = 0
    for y_ref in y_refs:
        part = _dot(y_ref[...].astype(BF16), w_ref[r0:r0 + y_ref.shape[1], :])
        mo = part if mo is None else mo + part
        r0 += y_ref.shape[1]
    out_ref[...] = x_ref[...] + _rms(mo, g_ref[...])


def _out_proj(ys, x, gain, w, tm):
    m, d = x.shape
    tm = min(tm, m)
    assert m % tm == 0 and sum(y.shape[1] for y in ys) == w.shape[0]
    row = lambda wd: pl.BlockSpec((tm, wd), lambda i: (i, 0))
    return pl.pallas_call(
        _out_proj_body,
        grid=(m // tm,),
        in_specs=[row(y.shape[1]) for y in ys] + [
            row(d),
            pl.BlockSpec((1, d), lambda i: (0, 0)),
            pl.BlockSpec(w.shape, lambda i: (0, 0), pipeline_mode=pl.Buffered(1)),
        ],
        out_specs=row(d),
        out_shape=jax.ShapeDtypeStruct((m, d), F32),
        compiler_params=_params(("parallel",)),
    )(*ys, x, gain.reshape(1, d), w)


def _mlp_body(x_ref, g_pre_ref, g_post_ref, w1_ref, w2_ref, out_ref, h_ref, acc_ref):
    j = pl.program_id(1)

    @pl.when(j == 0)
    def _():
        h_ref[...] = _rms(x_ref[...], g_pre_ref[...]).astype(BF16)
        acc_ref[...] = jnp.zeros_like(acc_ref)

    a = jnp.maximum(_dot(h_ref[...], w1_ref[...]), 0.0)
    acc_ref[...] += _dot((a * a).astype(BF16), w2_ref[...])

    @pl.when(j == pl.num_programs(1) - 1)
    def _():
        out_ref[...] = x_ref[...] + _rms(acc_ref[...], g_post_ref[...])


def _mlp(x, g_pre, g_post, w1, w2, tm, tf):
    m, d = x.shape
    ff = w1.shape[1]
    tm = min(tm, m)
    assert m % tm == 0 and ff % tf == 0
    return pl.pallas_call(
        _mlp_body,
        grid=(m // tm, ff // tf),
        in_specs=[
            pl.BlockSpec((tm, d), lambda i, j: (i, 0)),
            pl.BlockSpec((1, d), lambda i, j: (0, 0)),
            pl.BlockSpec((1, d), lambda i, j: (0, 0)),
            pl.BlockSpec((d, tf), lambda i, j: (0, j)),
            pl.BlockSpec((tf, d), lambda i, j: (j, 0)),
        ],
        out_specs=pl.BlockSpec((tm, d), lambda i, j: (i, 0)),
        out_shape=jax.ShapeDtypeStruct((m, d), F32),
        scratch_shapes=[pltpu.VMEM((tm, d), BF16), pltpu.VMEM((tm, d), F32)],
        compiler_params=_params(("parallel", "arbitrary")),
    )(x, g_pre.reshape(1, d), g_post.reshape(1, d), w1, w2)


def _log_gammas():
    return jnp.asarray(np.log(1.0 - 2.0 ** (-5.0 - np.arange(H_R, dtype=np.float64))), F32)


def _group_norm_gate(o, gate):
    mu = jnp.mean(o, axis=-1, keepdims=True)
    var = jnp.mean(jnp.square(o - mu), axis=-1, keepdims=True)
    o = (o - mu) * lax.rsqrt(var + GN_EPS)
    return gate / (1.0 + jnp.exp(-gate)) * o


def _ret_prompt_body(lg_ref, q_ref, k_ref, v_ref, gate_ref, y_ref, st_ref, s_ref):
    c = pl.program_id(1)

    @pl.when(c == 0)
    def _():
        s_ref[...] = jnp.zeros_like(s_ref)

    chunk = q_ref.shape[0]
    row = lax.broadcasted_iota(jnp.int32, (chunk, chunk), 0)
    col = lax.broadcasted_iota(jnp.int32, (chunk, chunk), 1)
    diff = (row - col).astype(F32)
    ii = lax.broadcasted_iota(jnp.int32, (chunk, 1), 0).astype(F32)
    for h in range(H_R):
        lg = lg_ref[h]
        decay = jnp.where(diff >= 0, jnp.exp(jnp.maximum(diff, 0.0) * lg), 0.0)
        q = q_ref[:, h * DK_R:(h + 1) * DK_R]
        k = k_ref[:, h * DK_R:(h + 1) * DK_R]
        v = v_ref[:, h * DV_R:(h + 1) * DV_R].astype(BF16)
        a = _dot_nt(q.astype(BF16), k.astype(BF16)) * decay
        s_prev = s_ref[h]
        o = _dot(a.astype(BF16), v) + _dot((q * jnp.exp((ii + 1.0) * lg)).astype(BF16), s_prev.astype(BF16))
        kd = (k * jnp.exp((chunk - 1.0 - ii) * lg)).astype(BF16)
        s_ref[h] = jnp.exp(jnp.full((1, 1), chunk, F32) * lg) * s_prev + _dot_tn(kd, v)
        y_ref[:, h * DV_R:(h + 1) * DV_R] = _group_norm_gate(o, gate_ref[:, h * DV_R:(h + 1) * DV_R])

    @pl.when(c == pl.num_programs(1) - 1)
    def _():
        st_ref[...] = s_ref[...]


def _ret_prompt(zr, nseq, seq):
    nc = seq // RET_CHUNK
    zv = zr.reshape(nseq, seq, zr.shape[1])
    qk_w, v_w = H_R * DK_R, H_R * DV_R
    assert v_w == 2 * qk_w
    y, st = pl.pallas_call(
        _ret_prompt_body,
        grid=(nseq, nc),
        in_specs=[
            pl.BlockSpec(memory_space=pltpu.SMEM),
            pl.BlockSpec((None, RET_CHUNK, qk_w), lambda n, c: (n, c, 0)),
            pl.BlockSpec((None, RET_CHUNK, qk_w), lambda n, c: (n, c, 1)),
            pl.BlockSpec((None, RET_CHUNK, v_w), lambda n, c: (n, c, 1)),
            pl.BlockSpec((None, RET_CHUNK, v_w), lambda n, c: (n, c, 2)),
        ],
        out_specs=[
            pl.BlockSpec((None, RET_CHUNK, v_w), lambda n, c: (n, c, 0)),
            pl.BlockSpec((None, H_R, DK_R, DV_R), lambda n, c: (n, 0, 0, 0)),
        ],
        out_shape=[
            jax.ShapeDtypeStruct((nseq, seq, v_w), F32),
            jax.ShapeDtypeStruct((nseq, H_R, DK_R, DV_R), F32),
        ],
        scratch_shapes=[pltpu.VMEM((H_R, DK_R, DV_R), F32)],
        compiler_params=_params(("parallel", "arbitrary")),
    )(_log_gammas(), zv, zv, zv, zv)
    return y.reshape(nseq * seq, v_w), st


def _ret_sample_body(*refs, t_new, has_prev):
    lg_ref, z_ref, s_in_ref = refs[:3]
    y_ref, s_out_ref, kpad_ref, vpad_ref = refs[4:] if has_prev else refs[3:]

    @pl.when(pl.program_id(0) == 0)
    def _():
        kpad_ref[...] = jnp.zeros_like(kpad_ref)
        vpad_ref[...] = jnp.zeros_like(vpad_ref)

    rows = 8
    row = lax.broadcasted_iota(jnp.int32, (rows, BLOCK), 0)
    col = lax.broadcasted_iota(jnp.int32, (rows, BLOCK), 1)
    diff = (row - col).astype(F32)
    ii = lax.broadcasted_iota(jnp.int32, (rows, 1), 0).astype(F32)
    ik = lax.broadcasted_iota(jnp.int32, (t_new, 1), 0).astype(F32)
    zq = jnp.zeros((rows - t_new, DK_R), F32)
    for h in range(H_R):
        lg = lg_ref[h]
        q = jnp.concatenate([z_ref[:, h * DK_R:(h + 1) * DK_R], zq], axis=0)
        k = z_ref[:, (H_R + h) * DK_R:(H_R + h + 1) * DK_R]
        v = z_ref[:, 2 * H_R * DK_R + h * DV_R:2 * H_R * DK_R + (h + 1) * DV_R]
        gate = z_ref[:, 2 * H_R * DK_R + (H_R + h) * DV_R:2 * H_R * DK_R + (H_R + h + 1) * DV_R]
        kpad_ref[0:t_new, :] = k
        vpad_ref[0:t_new, :] = v
        decay = jnp.where(diff >= 0, jnp.exp(jnp.maximum(diff, 0.0) * lg), 0.0)
        a = _dot_nt(q.astype(BF16), kpad_ref[...].astype(BF16)) * decay
        s_prev = s_in_ref[h]
        vb = vpad_ref[...].astype(BF16)
        o = _dot(a.astype(BF16), vb) + _dot((q * jnp.exp((ii + 1.0) * lg)).astype(BF16), s_prev.astype(BF16))
        kpad_ref[0:t_new, :] = k * jnp.exp((t_new - 1.0 - ik) * lg)
        upd = _dot_tn(kpad_ref[...].astype(BF16), vb)
        s_out_ref[h] = jnp.exp(jnp.full((1, 1), t_new, F32) * lg) * s_prev + upd
        y_ref[:, h * DV_R:(h + 1) * DV_R] = _group_norm_gate(o[0:t_new], gate)


def _ret_sample(zr, state, new_state, o, nreq, t_new):
    zv = zr.reshape(nreq, t_new, zr.shape[1])
    has_prev = new_state is not None
    state_blk = pl.BlockSpec((None, None, H_R, DK_R, DV_R), lambda b: (o, b, 0, 0, 0))
    in_specs = [
        pl.BlockSpec(memory_space=pltpu.SMEM),
        pl.BlockSpec((None, t_new, zr.shape[1]), lambda b: (b, 0, 0)),
        state_blk,
    ]
    args = [_log_gammas(), zv, state]
    if has_prev:
        in_specs.append(pl.BlockSpec(memory_space=pl.ANY))
        args.append(new_state)
    y, st = pl.pallas_call(
        functools.partial(_ret_sample_body, t_new=t_new, has_prev=has_prev),
        grid=(nreq,),
        in_specs=in_specs,
        out_specs=[pl.BlockSpec((None, t_new, H_R * DV_R), lambda b: (b, 0, 0)), state_blk],
        out_shape=[
            jax.ShapeDtypeStruct((nreq, t_new, H_R * DV_R), F32),
            jax.ShapeDtypeStruct(state.shape, F32),
        ],
        input_output_aliases={3: 1} if has_prev else {},
        scratch_shapes=[pltpu.VMEM((BLOCK, DK_R), F32), pltpu.VMEM((BLOCK, DV_R), F32)],
        compiler_params=_params(("arbitrary",)),
    )(*args)
    return y.reshape(nreq * t_new, H_R * DV_R), st


def kernel(x_prompt, x_sample, cache_dil_w128, cache_dil_w512, cache_dil_w2048, cache_swa, state_ret,
           norm_gains, w_in_mix, w_out_mix, attn_sinks, w_in_ret, w_out_ret, w_ff1, w_ff2):
    nseq, seq, d = x_prompt.shape
    nreq, t_new, _ = x_sample.shape
    depth = norm_gains.shape[0]
    assert seq % (BLOCK * DIL_PAIRS[-1][1]) == 0 and t_new <= DIL_PAIRS[1][1]
    xp = x_prompt.reshape(nseq * seq, d)
    xs = x_sample.reshape(nreq * t_new, d)
    dil_caches = (cache_dil_w128, cache_dil_w512, cache_dil_w2048)
    pos_p = np.arange(seq)
    pos_s = PAST_LEN + np.arange(t_new)
    rows_s = min(nreq * t_new, 512)
    rope_p = {dh: _rope_tables(pos_p, dh) for dh in (HEAD_DIM, DK_R)}
    rope_s = {dh: _rope_tables(np.tile(pos_s, rows_s // t_new), dh) for dh in (HEAD_DIM, DK_R)}
    mix_widths = (N_DIL * HW, 2 * HW, 2 * HW, 2 * HW, HW + 2 * KVB)

    rope_t = _rope_tables_t(pos_p)
    n_even = (depth + 1) // 2
    dil_p = [None for _ in DIL_PAIRS]
    swa_p = None
    dil_s = [[] for _ in DIL_PAIRS]
    swa_s, ret_p = [], []
    ret_s = None
    for layer in range(depth):
        gn = norm_gains[layer]
        if layer % 2 == 0:
            e = layer // 2
            w_in = _slot_order(w_in_mix[e], 1, N_DIL * 3 * HW).astype(BF16)
            w_out = _slot_order(w_out_mix[e], 0, HW).astype(BF16)
            qa_p, *kv_p, zb_p = _norm_proj(_mix_proj_body, xp, gn[0], w_in, *rope_p[HEAD_DIM], mix_widths, 256)
            qa_s, *kv_s, zb_s = _norm_proj(_mix_proj_body, xs, gn[0], w_in, *rope_s[HEAD_DIM], mix_widths, rows_s)
            kv_cols = lambda lo, hi: w_in_mix[e][:, lo:hi].T.astype(BF16)
            for g, (win, _) in enumerate(DIL_PAIRS):
                wt = kv_cols(g * 3 * HW + HW, (g + 1) * 3 * HW)
                dil_p[g] = _cache_rows(xp, gn[0], wt, *rope_t, dil_p[g], e, n_even, nseq, seq, min(win, seq), 512)
            wt = kv_cols(N_DIL * 3 * HW + HW, N_DIL * 3 * HW + HW + 2 * KVB)
            swa_p = _cache_rows(xp, gn[0], wt, *rope_t, swa_p, e, n_even, nseq, seq, min(SWA_WINDOW, seq), 512)
            oa_p, ob_p = _prompt_attn(qa_p, kv_p, zb_p, attn_sinks[e], nseq, seq)
            xp = _out_proj([oa_p, ob_p], xp, gn[1], w_out, 512)
            y_s = _sample_attn(qa_s, kv_s, zb_s, dil_caches, cache_swa, attn_sinks[e], e, nreq, t_new)
            xs = _out_proj([y_s], xs, gn[1], w_out, 512)
            for g in range(N_DIL):
                dil_s[g].append(kv_s[g].reshape(nreq, t_new, 2, N_HEADS, HEAD_DIM))
            swa_s.append(zb_s[:, HW:].reshape(nreq, t_new, 2, H_BKV, HEAD_DIM))
        else:
            o = layer // 2
            w_in = w_in_ret[o].astype(BF16)
            w_out = w_out_ret[o].astype(BF16)
            width = (w_in.shape[1],)
            zr_p, = _norm_proj(_ret_proj_body, xp, gn[0], w_in, *rope_p[DK_R], width, 256)
            zr_s, = _norm_proj(_ret_proj_body, xs, gn[0], w_in, *rope_s[DK_R], width, rows_s)
            y_p, st_p = _ret_prompt(zr_p, nseq, seq)
            y_s, ret_s = _ret_sample(zr_s, state_ret, ret_s, o, nreq, t_new)
            xp = _out_proj([y_p], xp, gn[1], w_out, 512)
            xs = _out_proj([y_s], xs, gn[1], w_out, 512)
            ret_p.append(st_p)
        w1 = w_ff1[layer].astype(BF16)
        w2 = w_ff2[layer].astype(BF16)
        xp = _mlp(xp, gn[2], gn[3], w1, w2, 1024, 1024)
        xs = _mlp(xs, gn[2], gn[3], w1, w2, 512, 512)
    def rows_out(buf, heads):
        n_l, n_s, _, wb = buf.shape
        return jnp.transpose(buf.reshape(n_l, n_s, 2, heads, HEAD_DIM, wb), (0, 1, 5, 2, 3, 4))

    return (xp.reshape(nseq, seq, d), xs.reshape(nreq, t_new, d),
            rows_out(dil_p[0], N_HEADS), jnp.stack(dil_s[0]),
            rows_out(dil_p[1], N_HEADS), jnp.stack(dil_s[1]),
            rows_out(dil_p[2], N_HEADS), jnp.stack(dil_s[2]),
            rows_out(swa_p, H_BKV), jnp.stack(swa_s),
            jnp.stack(ret_p), ret_s)
```

```python
import functools

import numpy as np
import jax
import jax.numpy as jnp
from jax import lax
from jax.experimental import pallas as pl
from jax.experimental.pallas import tpu as pltpu

F32 = jnp.float32
BF16 = jnp.bfloat16

HEAD_DIM = 64
ROPE_HALF = HEAD_DIM // 2
N_HEADS = 8
HW = N_HEADS * HEAD_DIM
DIL_PAIRS = ((128, 1), (512, 4), (2048, 16))
N_DIL = len(DIL_PAIRS)
H_BKV = 2
G_B = N_HEADS // H_BKV
KVB = H_BKV * HEAD_DIM
SWA_WINDOW = 128
H_R = 4
DK_R = 256
DV_R = 512
RET_CHUNK = 128
BLOCK = 128
PAST_LEN = 8192
ROPE_THETA = 10000.0
NORM_EPS = 1e-6
GN_EPS = 1e-5
NEG_INF = -1e30
LANES = 128
BLOCKS_IN_FLIGHT = 3
FIRST_BLOCKS_IN_FLIGHT = 4
RET_REQS_PER_STEP = 2
Q_PAD = 8
Q_SHIFT = Q_PAD.bit_length() - 1
HEAD_SHIFT = HEAD_DIM.bit_length() - 1
VMEM_LIMIT = 56 * 1024 * 1024


def _swa_slot(hq):
    return (hq % G_B) * H_BKV + hq // G_B


def _slot_order(w, axis, base):
    piece = lambda lo, hi: lax.slice_in_dim(w, lo, hi, axis=axis)
    q = piece(base, base + HW)
    shape = q.shape[:axis] + (H_BKV, G_B, HEAD_DIM) + q.shape[axis + 1:]
    q = jnp.swapaxes(q.reshape(shape), axis, axis + 1).reshape(q.shape)
    return jnp.concatenate([piece(0, base), q, piece(base + HW, w.shape[axis])], axis=axis)


def _params(sem):
    return pltpu.CompilerParams(dimension_semantics=sem, vmem_limit_bytes=VMEM_LIMIT)


def _rms(x, g):
    return x * lax.rsqrt(jnp.mean(x * x, axis=-1, keepdims=True) + NORM_EPS) * g


def _dot(a, b):
    return jnp.dot(a, b, preferred_element_type=F32)


def _dot_nt(a, b):
    return lax.dot_general(a, b, (((1,), (1,)), ((), ())), preferred_element_type=F32)


def _dot_tn(a, b):
    return lax.dot_general(a, b, (((0,), (0,)), ((), ())), preferred_element_type=F32)


def _rope_tables(pos, dh):
    half = dh // 2
    inv = ROPE_THETA ** (-np.arange(half, dtype=np.float64) * (2.0 / dh))
    ang = np.asarray(pos, np.float64)[:, None] * inv[None, :]
    c, s = np.cos(ang), np.sin(ang)
    if dh == HEAD_DIM:
        cos = np.tile(np.concatenate([c, c], axis=1), (1, LANES // dh))
        sin = np.tile(np.concatenate([-s, s], axis=1), (1, LANES // dh))
    else:
        cos, sin = c, s
    return jnp.asarray(cos, F32), jnp.asarray(sin, F32)


def _rope_tables_t(pos):
    inv = ROPE_THETA ** (-np.arange(ROPE_HALF, dtype=np.float64) * (2.0 / HEAD_DIM))
    ang = inv[:, None] * np.asarray(pos, np.float64)[None, :]
    return jnp.asarray(np.cos(ang), F32), jnp.asarray(np.sin(ang), F32)


def _cache_rows_body(*refs, rope_heads):
    x_ref, g_ref, wt_ref, cos_ref, sin_ref = refs[:5]
    out_ref = refs[-1]
    h = _rms(x_ref[...], g_ref[...]).astype(BF16)
    zt = _dot_nt(wt_ref[...], h)
    cos, sin = cos_ref[...], sin_ref[...]
    for hh in range(rope_heads):
        r0 = hh * HEAD_DIM
        k1, k2 = zt[r0:r0 + ROPE_HALF], zt[r0 + ROPE_HALF:r0 + HEAD_DIM]
        out_ref[r0:r0 + ROPE_HALF, :] = k1 * cos - k2 * sin
        out_ref[r0 + ROPE_HALF:r0 + HEAD_DIM, :] = k2 * cos + k1 * sin
    out_ref[rope_heads * HEAD_DIM:, :] = zt[rope_heads * HEAD_DIM:]


def _cache_rows(x, gain, wt, cos_t, sin_t, buf, e, n_layers, nseq, seq, wb, tm):
    d = x.shape[1]
    cols = wt.shape[0]
    tm = min(tm, wb)
    assert wb % tm == 0 and seq % tm == 0 and (seq - wb) % tm == 0
    tiles, first = seq // tm, (seq - wb) // tm
    in_specs = [
        pl.BlockSpec((tm, d), lambda n, j: (n * tiles + first + j, 0)),
        pl.BlockSpec((1, d), lambda n, j: (0, 0)),
        pl.BlockSpec((cols, d), lambda n, j: (0, 0)),
        pl.BlockSpec((ROPE_HALF, tm), lambda n, j: (0, first + j)),
        pl.BlockSpec((ROPE_HALF, tm), lambda n, j: (0, first + j)),
    ]
    args = [x, gain.reshape(1, d), wt, cos_t, sin_t]
    if buf is not None:
        in_specs.append(pl.BlockSpec(memory_space=pl.ANY))
        args.append(buf)
    return pl.pallas_call(
        functools.partial(_cache_rows_body, rope_heads=cols // (2 * HEAD_DIM)),
        grid=(nseq, wb // tm),
        in_specs=in_specs,
        out_specs=pl.BlockSpec((None, None, cols, tm), lambda n, j: (e, n, 0, j)),
        out_shape=jax.ShapeDtypeStruct((n_layers, nseq, cols, wb), F32),
        input_output_aliases={5: 0} if buf is not None else {},
        compiler_params=_params(("parallel", "parallel")),
    )(*args)


def _rope64(z, cos, sin):
    w = z.shape[1]
    reps = w // LANES
    lane = lax.broadcasted_iota(jnp.int32, z.shape, 1)
    first_half = (lane & (HEAD_DIM - 1)) < ROPE_HALF
    partner = jnp.where(first_half, pltpu.roll(z, w - ROPE_HALF, 1), pltpu.roll(z, ROPE_HALF, 1))
    if reps > 1:
        cos = jnp.concatenate([cos] * reps, axis=1)
        sin = jnp.concatenate([sin] * reps, axis=1)
    return z * cos + partner * sin


def _mix_proj_body(x_ref, g_ref, w_ref, cos_ref, sin_ref, qa_ref, kv0_ref, kv1_ref, kv2_ref, zb_ref):
    h = _rms(x_ref[...], g_ref[...]).astype(BF16)
    cos, sin = cos_ref[...], sin_ref[...]
    kv_refs = (kv0_ref, kv1_ref, kv2_ref)
    for g in range(N_DIL):
        base = g * 3 * HW
        qa_ref[:, g * HW:(g + 1) * HW] = _rope64(_dot(h, w_ref[:, base:base + HW]), cos, sin)
        kv_refs[g][:, :HW] = _rope64(_dot(h, w_ref[:, base + HW:base + 2 * HW]), cos, sin)
        kv_refs[g][:, HW:] = _dot(h, w_ref[:, base + 2 * HW:base + 3 * HW])
    base = N_DIL * 3 * HW
    zb_ref[:, :HW] = _rope64(_dot(h, w_ref[:, base:base + HW]), cos, sin)
    zb_ref[:, HW:HW + KVB] = _rope64(_dot(h, w_ref[:, base + HW:base + HW + KVB]), cos, sin)
    zb_ref[:, HW + KVB:] = _dot(h, w_ref[:, base + HW + KVB:base + HW + 2 * KVB])


def _ret_proj_body(x_ref, g_ref, w_ref, cos_ref, sin_ref, z_ref):
    h = _rms(x_ref[...], g_ref[...]).astype(BF16)
    cos, sin = cos_ref[...], sin_ref[...]
    half = DK_R // 2
    for part, scale in ((0, 1.0), (1, DK_R ** -0.5)):
        for hh in range(H_R):
            c0 = part * H_R * DK_R + hh * DK_R
            z = _dot(h, w_ref[:, c0:c0 + DK_R])
            z1, z2 = z[:, :half], z[:, half:]
            z_ref[:, c0:c0 + half] = ((z1 * cos - z2 * sin) * scale).astype(z_ref.dtype)
            z_ref[:, c0 + half:c0 + DK_R] = ((z2 * cos + z1 * sin) * scale).astype(z_ref.dtype)
    c0 = 2 * H_R * DK_R
    for j in range(2 * H_R):
        cols = slice(c0 + j * DV_R, c0 + (j + 1) * DV_R)
        z_ref[:, cols] = _dot(h, w_ref[:, cols]).astype(z_ref.dtype)


def _norm_proj(body, x, gain, w, cos, sin, out_widths, tm, out_dtype=F32):
    m, d = x.shape
    tm = min(tm, m)
    assert m % tm == 0 and cos.shape[0] % tm == 0
    period = cos.shape[0] // tm
    n = w.shape[1]
    outs = pl.pallas_call(
        body,
        grid=(m // tm,),
        in_specs=[
            pl.BlockSpec((tm, d), lambda i: (i, 0)),
            pl.BlockSpec((1, d), lambda i: (0, 0)),
            pl.BlockSpec((d, n), lambda i: (0, 0), pipeline_mode=pl.Buffered(1)),
            pl.BlockSpec((tm, LANES), lambda i: (i % period, 0)),
            pl.BlockSpec((tm, LANES), lambda i: (i % period, 0)),
        ],
        out_specs=[pl.BlockSpec((tm, wd), lambda i: (i, 0)) for wd in out_widths],
        out_shape=[jax.ShapeDtypeStruct((m, wd), out_dtype) for wd in out_widths],
        compiler_params=_params(("parallel",)),
    )(x, gain.reshape(1, d), w, cos, sin)
    return outs


def _attn_pairs(blocks, mask, sinks):
    lane = lax.broadcasted_iota(jnp.int32, (BLOCK, LANES), 1)
    first = lane < HEAD_DIM
    owns = (first, jnp.logical_not(first))
    both = [(b, half) for b in range(len(blocks)) for half in range(len(owns))]
    kbs = [k.astype(BF16) for _, k, _ in blocks]
    vbs = [jnp.concatenate([v.astype(BF16), jnp.ones(v.shape, BF16)], axis=1) for _, _, v in blocks]
    s = [jnp.where(mask, _dot_nt(jnp.where(owns[half], blocks[b][0], 0.0).astype(BF16), kbs[b]), NEG_INF)
         for b, half in both]
    m = [jnp.max(x, axis=-1, keepdims=True) for x in s]
    if sinks is not None:
        m = [jnp.maximum(x, sinks[half]) for x, (_, half) in zip(m, both)]
    p = [jnp.exp(x - y).astype(BF16) for x, y in zip(s, m)]
    pv_l = [_dot(x, vbs[b]) for x, (b, _) in zip(p, both)]
    out = []
    for b in range(len(blocks)):
        i0, i1 = len(owns) * b, len(owns) * b + 1
        l0, l1 = pv_l[i0][:, LANES:], pv_l[i1][:, LANES:]
        if sinks is not None:
            l0 = l0 + jnp.exp(sinks[0] - m[i0])
            l1 = l1 + jnp.exp(sinks[1] - m[i1])
        out.append((jnp.where(first, m[i0], m[i1]), jnp.where(first, l0, l1),
                    jnp.where(first, pv_l[i0][:, :LANES], pv_l[i1][:, :LANES])))
    return out


def _prompt_attn_body(sink_ref, q0_ref, k0_ref, v0_ref, q1_ref, k1_ref, v1_ref, q2_ref, k2_ref, v2_ref,
                      qb_ref, kb_ref, vb_ref, oa_ref, ob_ref, acc_ref, m_ref, l_ref, *, seq):
    pair = pl.program_id(1)
    scale = HEAD_DIM ** -0.5
    row = lax.broadcasted_iota(jnp.int32, (BLOCK, 2 * BLOCK), 0)
    col = lax.broadcasted_iota(jnp.int32, (BLOCK, 2 * BLOCK), 1)
    causal = (lax.broadcasted_iota(jnp.int32, (BLOCK, BLOCK), 1)
              <= lax.broadcasted_iota(jnp.int32, (BLOCK, BLOCK), 0))

    def window_mask(min_back):
        return jnp.where(col < BLOCK, col - row - min_back, row - (col - BLOCK)) >= 0

    def rows(ref, start, n, stride):
        return ref[pl.ds(start, n, stride=stride), :] if stride > 1 else ref[pl.ds(start, n), :]

    def group_pass(q_ref, k_ref, v_ref, dil, min_back, sinks, emit):
        span = dil * BLOCK
        nblk = seq // span

        def run(starts, back, keys, mask):
            blocks = [(rows(q_ref, s, BLOCK, dil) * scale, rows(k_ref, s - back, keys, dil),
                       rows(v_ref, s - back, keys, dil)) for s in starts]
            for s, mlp in zip(starts, _attn_pairs(blocks, mask, sinks)):
                emit(s, mlp)

        for r0 in range(0, dil, FIRST_BLOCKS_IN_FLIGHT):
            run(list(range(r0, min(r0 + FIRST_BLOCKS_IN_FLIGHT, dil))), 0, BLOCK, causal)
        if nblk == 1:
            return
        mask = window_mask(min_back)
        per_iter = BLOCKS_IN_FLIGHT if (nblk - 1) % BLOCKS_IN_FLIGHT == 0 else 1
        trips = (nblk - 1) // per_iter

        def block_start(r, lb):
            return lb * span + r if isinstance(lb, int) else pl.multiple_of(lb * span, span) + r

        for r in range(dil):
            def body(it, carry, r=r):
                run([block_start(r, 1 + it * per_iter + j) for j in range(per_iter)], span, 2 * BLOCK, mask)
                return carry

            if trips == 1:
                body(0, 0)
            else:
                lax.fori_loop(0, trips, body, 0)

    def merge_emit(dil, last):
        def emit(start, mlp):
            m, l, pv = mlp
            idx = pl.ds(start, BLOCK, stride=dil)
            m_old = m_ref[idx, :]
            m_new = jnp.maximum(m_old, m)
            a = jnp.exp(m_old - m_new)
            b = jnp.exp(m - m_new)
            acc = a * acc_ref[idx, :] + b * pv
            den = a * l_ref[idx, :] + b * l
            if last:
                oa_ref[idx, :] = acc / den
            else:
                m_ref[idx, :] = m_new
                acc_ref[idx, :] = acc
                l_ref[idx, :] = den
        return emit

    def first_emit(start, mlp):
        idx = pl.ds(start, BLOCK)
        m_ref[idx, :], l_ref[idx, :], acc_ref[idx, :] = mlp

    groups = ((q0_ref, k0_ref, v0_ref), (q1_ref, k1_ref, v1_ref), (q2_ref, k2_ref, v2_ref))
    for g, (win, dil) in enumerate(DIL_PAIRS):
        emit = first_emit if g == 0 else merge_emit(dil, last=g == N_DIL - 1)
        group_pass(*groups[g], dil, BLOCK - win // dil, None, emit)

    sinks = [sink_ref[0, pair + G_B * half] for half in range(H_BKV)]

    def swa_emit(start, mlp):
        m, l, pv = mlp
        ob_ref[pl.ds(start, BLOCK), :] = pv / l

    group_pass(qb_ref, kb_ref, vb_ref, 1, BLOCK + 1 - SWA_WINDOW, sinks, swa_emit)


def _prompt_attn(qa, kvs, zb, sinks, nseq, seq):
    assert seq % (BLOCK * DIL_PAIRS[-1][1]) == 0 and KVB == LANES
    pairs = HW // LANES
    view = lambda a: a.reshape(nseq, seq, a.shape[1])
    col = lambda j: pl.BlockSpec((None, seq, LANES), lambda n, p: (n, 0, j(p)))
    in_specs = [pl.BlockSpec(memory_space=pltpu.SMEM)]
    args = [sinks.reshape(1, N_HEADS)]
    for g in range(N_DIL):
        in_specs += [col(lambda p, g=g: g * pairs + p), col(lambda p: p), col(lambda p: pairs + p)]
        args += [view(qa), view(kvs[g]), view(kvs[g])]
    in_specs += [col(lambda p: p), col(lambda p: pairs), col(lambda p: pairs + 1)]
    args += [view(zb)] * 3
    out_spec = pl.BlockSpec((None, seq, LANES), lambda n, p: (n, 0, p))
    oa, ob = pl.pallas_call(
        functools.partial(_prompt_attn_body, seq=seq),
        grid=(nseq, pairs),
        in_specs=in_specs,
        out_specs=[out_spec, out_spec],
        out_shape=[jax.ShapeDtypeStruct((nseq, seq, HW), F32)] * 2,
        scratch_shapes=[pltpu.VMEM((seq, LANES), F32)] * 3,
        compiler_params=_params(("parallel", "parallel")),
    )(*args)
    return oa.reshape(nseq * seq, HW), ob.reshape(nseq * seq, HW)


def _sample_attn_body(qa_ref, kn0_ref, kn1_ref, kn2_ref, zb_ref, c0_ref, c1_ref, c2_ref, cs_ref, sink_ref,
                      y_ref, pvn_ref, *, t_new):
    scale = HEAD_DIM ** -0.5
    r_i = lax.broadcasted_iota(jnp.int32, (N_HEADS * Q_PAD, HW), 0)
    l_i = lax.broadcasted_iota(jnp.int32, (N_HEADS * Q_PAD, HW), 1)
    own_head = (r_i >> Q_SHIFT) == (l_i >> HEAD_SHIFT)
    t_q = lax.broadcasted_iota(jnp.int32, (Q_PAD, 1), 0)
    new_key = lax.broadcasted_iota(jnp.int32, (Q_PAD, BLOCK), 1)

    def pad_rows(x, n):
        return jnp.concatenate([x, jnp.zeros((n - x.shape[0], x.shape[1]), x.dtype)], axis=0)

    def head_cols(ref, base, h):
        return ref[:, base + h * HEAD_DIM:base + (h + 1) * HEAD_DIM]

    kn_refs = (kn0_ref, kn1_ref, kn2_ref)
    c_refs = (c0_ref, c1_ref, c2_ref)
    stats = [[] for _ in range(N_HEADS)]
    for g, (win, dil) in enumerate(DIL_PAIRS):
        c_ref = c_refs[g]
        wb = c_ref.shape[-1]
        q_all = pad_rows(qa_ref[:, g * HW:(g + 1) * HW] * scale, Q_PAD)
        q_exp = jnp.where(own_head, jnp.concatenate([q_all] * N_HEADS, axis=0), 0.0).astype(BF16)
        s_new_all = _dot_nt(q_exp, pad_rows(kn_refs[g][:, :HW].astype(BF16), BLOCK))
        back = t_q - new_key
        mask_new = jnp.logical_and(jnp.logical_and(back >= 0, new_key < t_new), (back & (dil - 1)) == 0)
        ahead = lax.broadcasted_iota(jnp.int32, (Q_PAD, wb), 1) - t_q
        mask_cache = jnp.logical_and(ahead >= 0, (ahead & (dil - 1)) == 0)
        p_news = []
        partial = []
        for h in range(N_HEADS):
            qh = pad_rows(head_cols(qa_ref, g * HW, h) * scale, Q_PAD).astype(BF16)
            s_c = jnp.where(mask_cache, _dot(qh, c_ref[0, h].astype(BF16)), NEG_INF)
            s_n = jnp.where(mask_new, s_new_all[h * Q_PAD:(h + 1) * Q_PAD], NEG_INF)
            m = jnp.maximum(jnp.max(s_c, axis=-1, keepdims=True), jnp.max(s_n, axis=-1, keepdims=True))
            p_c = jnp.exp(s_c - m)
            p_n = jnp.exp(s_n - m)
            l = jnp.sum(p_c, axis=-1, keepdims=True) + jnp.sum(p_n, axis=-1, keepdims=True)
            partial.append((m, l, _dot_nt(p_c.astype(BF16), c_ref[1, h].astype(BF16))))
            p_news.append(p_n.astype(BF16))
        pvn_ref[...] = _dot(jnp.concatenate(p_news, axis=0), pad_rows(kn_refs[g][:, HW:].astype(BF16), BLOCK))
        for h, (m, l, pv) in enumerate(partial):
            pv = pv + pvn_ref[h * Q_PAD:(h + 1) * Q_PAD, h * HEAD_DIM:(h + 1) * HEAD_DIM]
            stats[h].append((m, l, pv))
    for h in range(N_HEADS):
        m_all = functools.reduce(jnp.maximum, [s[0] for s in stats[h]])
        acc = jnp.zeros((Q_PAD, HEAD_DIM), F32)
        den = jnp.zeros((Q_PAD, 1), F32)
        for m, l, pv in stats[h]:
            a = jnp.exp(m - m_all)
            acc = acc + a * pv
            den = den + a * l
        y_ref[:, h * HEAD_DIM:(h + 1) * HEAD_DIM] = (acc / den)[0:t_new]

    t_g = lax.broadcasted_iota(jnp.int32, (G_B * Q_PAD, 1), 0) & (Q_PAD - 1)
    key_g = lax.broadcasted_iota(jnp.int32, (G_B * Q_PAD, BLOCK), 1)
    mask_cache = key_g >= t_g + 1 + (BLOCK - SWA_WINDOW)
    mask_new = jnp.logical_and(t_g - key_g >= 0, key_g < t_new)
    for hk in range(H_BKV):
        qs = jnp.concatenate([pad_rows(head_cols(zb_ref, 0, _swa_slot(hk * G_B + gq)) * scale, Q_PAD)
                              for gq in range(G_B)], axis=0).astype(BF16)
        k_new = pad_rows(head_cols(zb_ref, HW, hk).astype(BF16), BLOCK)
        v_new = pad_rows(head_cols(zb_ref, HW + KVB, hk).astype(BF16), BLOCK)
        s_c = jnp.where(mask_cache, _dot(qs, cs_ref[0, hk].astype(BF16)), NEG_INF)
        s_n = jnp.where(mask_new, _dot_nt(qs, k_new), NEG_INF)
        sink = sink_ref[hk * G_B * Q_PAD:(hk + 1) * G_B * Q_PAD, 0:1]
        m = jnp.maximum(jnp.maximum(jnp.max(s_c, axis=-1, keepdims=True), jnp.max(s_n, axis=-1, keepdims=True)), sink)
        p_c = jnp.exp(s_c - m)
        p_n = jnp.exp(s_n - m)
        l = jnp.sum(p_c, axis=-1, keepdims=True) + jnp.sum(p_n, axis=-1, keepdims=True) + jnp.exp(sink - m)
        o = (_dot_nt(p_c.astype(BF16), cs_ref[1, hk].astype(BF16)) + _dot(p_n.astype(BF16), v_new)) / l
        for gq in range(G_B):
            slot = _swa_slot(hk * G_B + gq)
            y_ref[:, HW + slot * HEAD_DIM:HW + (slot + 1) * HEAD_DIM] = o[gq * Q_PAD:gq * Q_PAD + t_new]


def _sample_attn(qa, kns, zb, caches, cache_swa, sinks, e, nreq, t_new):
    assert t_new <= Q_PAD
    qv = qa.reshape(nreq, t_new, N_DIL * HW)
    knv = [k.reshape(nreq, t_new, 2 * HW) for k in kns]
    zbv = zb.reshape(nreq, t_new, HW + 2 * KVB)
    to_native = lambda c: jnp.transpose(c, (0, 1, 3, 4, 5, 2))
    cv = []
    c_specs = []
    for (win, dil), c in zip(DIL_PAIRS, caches):
        wb = c.shape[2]
        assert wb == win and wb // dil == BLOCK
        cv.append(to_native(c))
        c_specs.append(pl.BlockSpec((None, None, 2, N_HEADS, HEAD_DIM, wb), lambda b: (e, b, 0, 0, 0, 0)))
    assert cache_swa.shape[2] == BLOCK
    sink_rows = jnp.broadcast_to(jnp.repeat(sinks, Q_PAD)[:, None], (N_HEADS * Q_PAD, LANES))
    tok = lambda wd: pl.BlockSpec((None, t_new, wd), lambda b: (b, 0, 0))
    out = pl.pallas_call(
        functools.partial(_sample_attn_body, t_new=t_new),
        grid=(nreq,),
        in_specs=[tok(N_DIL * HW), tok(2 * HW), tok(2 * HW), tok(2 * HW), tok(HW + 2 * KVB)] + c_specs + [
            pl.BlockSpec((None, None, 2, H_BKV, HEAD_DIM, BLOCK), lambda b: (e, b, 0, 0, 0, 0)),
            pl.BlockSpec((N_HEADS * Q_PAD, LANES), lambda b: (0, 0)),
        ],
        out_specs=tok(2 * HW),
        out_shape=jax.ShapeDtypeStruct((nreq, t_new, 2 * HW), F32),
        scratch_shapes=[pltpu.VMEM((N_HEADS * Q_PAD, HW), F32)],
        compiler_params=_params(("parallel",)),
    )(qv, *knv, zbv, *cv, to_native(cache_swa), sink_rows)
    return out.reshape(nreq * t_new, 2 * HW)


def _out_proj_body(*refs):
    *y_refs, x_ref, g_ref, w_ref, out_ref = refs
    mo = None
    r0 = 0
    for y_ref in y_refs:
        part = _dot(y_ref[...].astype(BF16), w_ref[r0:r0 + y_ref.shape[1], :])
        mo = part if mo is None else mo + part
        r0 += y_ref.shape[1]
    out_ref[...] = x_ref[...] + _rms(mo, g_ref[...])


def _out_proj(ys, x, gain, w, tm):
    m, d = x.shape
    tm = min(tm, m)
    assert m % tm == 0 and sum(y.shape[1] for y in ys) == w.shape[0]
    row = lambda wd: pl.BlockSpec((tm, wd), lambda i: (i, 0))
    return pl.pallas_call(
        _out_proj_body,
        grid=(m // tm,),
        in_specs=[row(y.shape[1]) for y in ys] + [
            row(d),
            pl.BlockSpec((1, d), lambda i: (0, 0)),
            pl.BlockSpec(w.shape, lambda i: (0, 0), pipeline_mode=pl.Buffered(1)),
        ],
        out_specs=row(d),
        out_shape=jax.ShapeDtypeStruct((m, d), F32),
        compiler_params=_params(("parallel",)),
    )(*ys, x, gain.reshape(1, d), w)


def _mlp_body(x_ref, g_pre_ref, g_post_ref, w1_ref, w2_ref, out_ref, h_ref, acc_ref):
    j = pl.program_id(1)

    @pl.when(j == 0)
    def _():
        h_ref[...] = _rms(x_ref[...], g_pre_ref[...]).astype(BF16)
        acc_ref[...] = jnp.zeros_like(acc_ref)

    a = jnp.maximum(_dot(h_ref[...], w1_ref[...]), 0.0)
    acc_ref[...] += _dot((a * a).astype(BF16), w2_ref[...])

    @pl.when(j == pl.num_programs(1) - 1)
    def _():
        out_ref[...] = x_ref[...] + _rms(acc_ref[...], g_post_ref[...])


def _mlp(x, g_pre, g_post, w1, w2, tm, tf):
    m, d = x.shape
    ff = w1.shape[1]
    tm = min(tm, m)
    assert m % tm == 0 and ff % tf == 0
    return pl.pallas_call(
        _mlp_body,
        grid=(m // tm, ff // tf),
        in_specs=[
            pl.BlockSpec((tm, d), lambda i, j: (i, 0)),
            pl.BlockSpec((1, d), lambda i, j: (0, 0)),
            pl.BlockSpec((1, d), lambda i, j: (0, 0)),
            pl.BlockSpec((d, tf), lambda i, j: (0, j)),
            pl.BlockSpec((tf, d), lambda i, j: (j, 0)),
        ],
        out_specs=pl.BlockSpec((tm, d), lambda i, j: (i, 0)),
        out_shape=jax.ShapeDtypeStruct((m, d), F32),
        scratch_shapes=[pltpu.VMEM((tm, d), BF16), pltpu.VMEM((tm, d), F32)],
        compiler_params=_params(("parallel", "arbitrary")),
    )(x, g_pre.reshape(1, d), g_post.reshape(1, d), w1, w2)


def _log_gammas():
    return jnp.asarray(np.log(1.0 - 2.0 ** (-5.0 - np.arange(H_R, dtype=np.float64))), F32)


def _group_norm_gate(o, gate):
    mu = jnp.mean(o, axis=-1, keepdims=True)
    var = jnp.mean(jnp.square(o - mu), axis=-1, keepdims=True)
    o = (o - mu) * lax.rsqrt(var + GN_EPS)
    return gate / (1.0 + jnp.exp(-gate)) * o


def _ret_prompt_body(lg_ref, q_ref, k_ref, v_ref, gate_ref, y_ref, st_ref, s_ref):
    c = pl.program_id(1)

    @pl.when(c == 0)
    def _():
        s_ref[...] = jnp.zeros_like(s_ref)

    chunk = q_ref.shape[0]
    row = lax.broadcasted_iota(jnp.int32, (chunk, chunk), 0)
    col = lax.broadcasted_iota(jnp.int32, (chunk, chunk), 1)
    diff = (row - col).astype(F32)
    ii = lax.broadcasted_iota(jnp.int32, (chunk, 1), 0).astype(F32)
    for h in range(H_R):
        lg = lg_ref[h]
        decay = jnp.where(diff >= 0, jnp.exp(jnp.maximum(diff, 0.0) * lg), 0.0)
        q = q_ref[:, h * DK_R:(h + 1) * DK_R]
        k = k_ref[:, h * DK_R:(h + 1) * DK_R]
        v = v_ref[:, h * DV_R:(h + 1) * DV_R].astype(BF16)
        a = _dot_nt(q.astype(BF16), k.astype(BF16)) * decay
        s_prev = s_ref[h]
        o = _dot(a.astype(BF16), v) + _dot((q * jnp.exp((ii + 1.0) * lg)).astype(BF16), s_prev.astype(BF16))
        kd = (k * jnp.exp((chunk - 1.0 - ii) * lg)).astype(BF16)
        s_ref[h] = jnp.exp(jnp.full((1, 1), chunk, F32) * lg) * s_prev + _dot_tn(kd, v)
        gate = gate_ref[:, h * DV_R:(h + 1) * DV_R].astype(F32)
        y_ref[:, h * DV_R:(h + 1) * DV_R] = _group_norm_gate(o, gate).astype(y_ref.dtype)

    @pl.when(c == pl.num_programs(1) - 1)
    def _():
        st_ref[...] = s_ref[...]


def _ret_prompt(zr, nseq, seq):
    nc = seq // RET_CHUNK
    zv = zr.reshape(nseq, seq, zr.shape[1])
    qk_w, v_w = H_R * DK_R, H_R * DV_R
    assert v_w == 2 * qk_w
    y, st = pl.pallas_call(
        _ret_prompt_body,
        grid=(nseq, nc),
        in_specs=[
            pl.BlockSpec(memory_space=pltpu.SMEM),
            pl.BlockSpec((None, RET_CHUNK, qk_w), lambda n, c: (n, c, 0)),
            pl.BlockSpec((None, RET_CHUNK, qk_w), lambda n, c: (n, c, 1)),
            pl.BlockSpec((None, RET_CHUNK, v_w), lambda n, c: (n, c, 1)),
            pl.BlockSpec((None, RET_CHUNK, v_w), lambda n, c: (n, c, 2)),
        ],
        out_specs=[
            pl.BlockSpec((None, RET_CHUNK, v_w), lambda n, c: (n, c, 0)),
            pl.BlockSpec((None, H_R, DK_R, DV_R), lambda n, c: (n, 0, 0, 0)),
        ],
        out_shape=[
            jax.ShapeDtypeStruct((nseq, seq, v_w), BF16),
            jax.ShapeDtypeStruct((nseq, H_R, DK_R, DV_R), F32),
        ],
        scratch_shapes=[pltpu.VMEM((H_R, DK_R, DV_R), F32)],
        compiler_params=_params(("parallel", "arbitrary")),
    )(_log_gammas(), zv, zv, zv, zv)
    return y.reshape(nseq * seq, v_w), st


def _ret_sample_body(*refs, t_new, has_prev):
    lg_ref, z_ref, s_in_ref = refs[:3]
    y_ref, s_out_ref, kpad_ref, vpad_ref = refs[4:] if has_prev else refs[3:]

    @pl.when(pl.program_id(0) == 0)
    def _():
        kpad_ref[...] = jnp.zeros_like(kpad_ref)
        vpad_ref[...] = jnp.zeros_like(vpad_ref)

    rows = 8
    row = lax.broadcasted_iota(jnp.int32, (rows, BLOCK), 0)
    col = lax.broadcasted_iota(jnp.int32, (rows, BLOCK), 1)
    diff = (row - col).astype(F32)
    ii = lax.broadcasted_iota(jnp.int32, (rows, 1), 0).astype(F32)
    ik = lax.broadcasted_iota(jnp.int32, (t_new, 1), 0).astype(F32)
    zq = jnp.zeros((rows - t_new, DK_R), F32)
    for b, h in [(b, h) for b in range(z_ref.shape[0]) for h in range(H_R)]:
        lg = lg_ref[h]
        q = jnp.concatenate([z_ref[b, :, h * DK_R:(h + 1) * DK_R], zq], axis=0)
        k = z_ref[b, :, (H_R + h) * DK_R:(H_R + h + 1) * DK_R]
        v = z_ref[b, :, 2 * H_R * DK_R + h * DV_R:2 * H_R * DK_R + (h + 1) * DV_R]
        gate = z_ref[b, :, 2 * H_R * DK_R + (H_R + h) * DV_R:2 * H_R * DK_R + (H_R + h + 1) * DV_R]
        kpad_ref[0:t_new, :] = k
        vpad_ref[0:t_new, :] = v
        decay = jnp.where(diff >= 0, jnp.exp(jnp.maximum(diff, 0.0) * lg), 0.0)
        a = _dot_nt(q.astype(BF16), kpad_ref[...].astype(BF16)) * decay
        s_prev = s_in_ref[b, h]
        vb = vpad_ref[...].astype(BF16)
        o = _dot(a.astype(BF16), vb) + _dot((q * jnp.exp((ii + 1.0) * lg)).astype(BF16), s_prev.astype(BF16))
        kpad_ref[0:t_new, :] = k * jnp.exp((t_new - 1.0 - ik) * lg)
        upd = _dot_tn(kpad_ref[...].astype(BF16), vb)
        s_out_ref[b, h] = jnp.exp(jnp.full((1, 1), t_new, F32) * lg) * s_prev + upd
        y_ref[b, :, h * DV_R:(h + 1) * DV_R] = _group_norm_gate(o[0:t_new], gate)


def _ret_sample(zr, state, new_state, o, nreq, t_new):
    zv = zr.reshape(nreq, t_new, zr.shape[1])
    has_prev = new_state is not None
    nb = RET_REQS_PER_STEP if nreq % RET_REQS_PER_STEP == 0 else 1
    state_blk = pl.BlockSpec((None, nb, H_R, DK_R, DV_R), lambda b: (o, b, 0, 0, 0))
    in_specs = [
        pl.BlockSpec(memory_space=pltpu.SMEM),
        pl.BlockSpec((nb, t_new, zr.shape[1]), lambda b: (b, 0, 0)),
        state_blk,
    ]
    args = [_log_gammas(), zv, state]
    if has_prev:
        in_specs.append(pl.BlockSpec(memory_space=pl.ANY))
        args.append(new_state)
    y, st = pl.pallas_call(
        functools.partial(_ret_sample_body, t_new=t_new, has_prev=has_prev),
        grid=(nreq // nb,),
        in_specs=in_specs,
        out_specs=[pl.BlockSpec((nb, t_new, H_R * DV_R), lambda b: (b, 0, 0)), state_blk],
        out_shape=[
            jax.ShapeDtypeStruct((nreq, t_new, H_R * DV_R), F32),
            jax.ShapeDtypeStruct(state.shape, F32),
        ],
        input_output_aliases={3: 1} if has_prev else {},
        scratch_shapes=[pltpu.VMEM((BLOCK, DK_R), F32), pltpu.VMEM((BLOCK, DV_R), F32)],
        compiler_params=_params(("arbitrary",)),
    )(*args)
    return y.reshape(nreq * t_new, H_R * DV_R), st


def kernel(x_prompt, x_sample, cache_dil_w128, cache_dil_w512, cache_dil_w2048, cache_swa, state_ret,
           norm_gains, w_in_mix, w_out_mix, attn_sinks, w_in_ret, w_out_ret, w_ff1, w_ff2):
    nseq, seq, d = x_prompt.shape
    nreq, t_new, _ = x_sample.shape
    depth = norm_gains.shape[0]
    assert seq % (BLOCK * DIL_PAIRS[-1][1]) == 0 and t_new <= DIL_PAIRS[1][1]
    xp = x_prompt.reshape(nseq * seq, d)
    xs = x_sample.reshape(nreq * t_new, d)
    dil_caches = (cache_dil_w128, cache_dil_w512, cache_dil_w2048)
    pos_p = np.arange(seq)
    pos_s = PAST_LEN + np.arange(t_new)
    rows_s = min(nreq * t_new, 512)
    rope_p = {dh: _rope_tables(pos_p, dh) for dh in (HEAD_DIM, DK_R)}
    rope_s = {dh: _rope_tables(np.tile(pos_s, rows_s // t_new), dh) for dh in (HEAD_DIM, DK_R)}
    mix_widths = (N_DIL * HW, 2 * HW, 2 * HW, 2 * HW, HW + 2 * KVB)

    rope_t = _rope_tables_t(pos_p)
    n_even = (depth + 1) // 2
    dil_p = [None for _ in DIL_PAIRS]
    swa_p = None
    dil_s = [[] for _ in DIL_PAIRS]
    swa_s, ret_p = [], []
    ret_s = None
    for layer in range(depth):
        gn = norm_gains[layer]
        if layer % 2 == 0:
            e = layer // 2
            w_in = _slot_order(w_in_mix[e], 1, N_DIL * 3 * HW).astype(BF16)
            w_out = _slot_order(w_out_mix[e], 0, HW).astype(BF16)
            qa_p, *kv_p, zb_p = _norm_proj(_mix_proj_body, xp, gn[0], w_in, *rope_p[HEAD_DIM], mix_widths, 256)
            qa_s, *kv_s, zb_s = _norm_proj(_mix_proj_body, xs, gn[0], w_in, *rope_s[HEAD_DIM], mix_widths, rows_s)
            kv_cols = lambda lo, hi: w_in_mix[e][:, lo:hi].T.astype(BF16)
            for g, (win, _) in enumerate(DIL_PAIRS):
                wt = kv_cols(g * 3 * HW + HW, (g + 1) * 3 * HW)
                dil_p[g] = _cache_rows(xp, gn[0], wt, *rope_t, dil_p[g], e, n_even, nseq, seq, min(win, seq), 512)
            wt = kv_cols(N_DIL * 3 * HW + HW, N_DIL * 3 * HW + HW + 2 * KVB)
            swa_p = _cache_rows(xp, gn[0], wt, *rope_t, swa_p, e, n_even, nseq, seq, min(SWA_WINDOW, seq), 512)
            oa_p, ob_p = _prompt_attn(qa_p, kv_p, zb_p, attn_sinks[e], nseq, seq)
            xp = _out_proj([oa_p, ob_p], xp, gn[1], w_out, 512)
            y_s = _sample_attn(qa_s, kv_s, zb_s, dil_caches, cache_swa, attn_sinks[e], e, nreq, t_new)
            xs = _out_proj([y_s], xs, gn[1], w_out, 512)
            for g in range(N_DIL):
                dil_s[g].append(kv_s[g].reshape(nreq, t_new, 2, N_HEADS, HEAD_DIM))
            swa_s.append(zb_s[:, HW:].reshape(nreq, t_new, 2, H_BKV, HEAD_DIM))
        else:
            o = layer // 2
            w_in = w_in_ret[o].astype(BF16)
            w_out = w_out_ret[o].astype(BF16)
            width = (w_in.shape[1],)
            zr_p, = _norm_proj(_ret_proj_body, xp, gn[0], w_in, *rope_p[DK_R], width, 256, out_dtype=BF16)
            zr_s, = _norm_proj(_ret_proj_body, xs, gn[0], w_in, *rope_s[DK_R], width, rows_s)
            y_p, st_p = _ret_prompt(zr_p, nseq, seq)
            y_s, ret_s = _ret_sample(zr_s, state_ret, ret_s, o, nreq, t_new)
            xp = _out_proj([y_p], xp, gn[1], w_out, 512)
            xs = _out_proj([y_s], xs, gn[1], w_out, 512)
            ret_p.append(st_p)
        w1 = w_ff1[layer].astype(BF16)
        w2 = w_ff2[layer].astype(BF16)
        xp = _mlp(xp, gn[2], gn[3], w1, w2, 1024, 1024)
        xs = _mlp(xs, gn[2], gn[3], w1, w2, 512, 512)
    def rows_out(buf, heads):
        n_l, n_s, _, wb = buf.shape
        return jnp.transpose(buf.reshape(n_l, n_s, 2, heads, HEAD_DIM, wb), (0, 1, 5, 2, 3, 4))

    return (xp.reshape(nseq, seq, d), xs.reshape(nreq, t_new, d),
            rows_out(dil_p[0], N_HEADS), jnp.stack(dil_s[0]),
            rows_out(dil_p[1], N_HEADS), jnp.stack(dil_s[1]),
            rows_out(dil_p[2], N_HEADS), jnp.stack(dil_s[2]),
            rows_out(swa_p, H_BKV), jnp.stack(swa_s),
            jnp.stack(ret_p), ret_s)
```

```python
import functools

import numpy as np
import jax
import jax.numpy as jnp
from jax import lax
from jax.experimental import pallas as pl
from jax.experimental.pallas import tpu as pltpu

F32 = jnp.float32
BF16 = jnp.bfloat16

HEAD_DIM = 64
ROPE_HALF = HEAD_DIM // 2
N_HEADS = 8
HW = N_HEADS * HEAD_DIM
DIL_PAIRS = ((128, 1), (512, 4), (2048, 16))
N_DIL = len(DIL_PAIRS)
H_BKV = 2
G_B = N_HEADS // H_BKV
KVB = H_BKV * HEAD_DIM
SWA_WINDOW = 128
H_R = 4
DK_R = 256
DV_R = 512
RET_CHUNK = 128
BLOCK = 128
PAST_LEN = 8192
ROPE_THETA = 10000.0
NORM_EPS = 1e-6
GN_EPS = 1e-5
NEG_INF = -1e30
LANES = 128
BLOCKS_IN_FLIGHT = 3
FIRST_BLOCKS_IN_FLIGHT = 4
N_MLP_IN = 5
Q_PAD = 8
Q_SHIFT = Q_PAD.bit_length() - 1
HEAD_SHIFT = HEAD_DIM.bit_length() - 1
VMEM_LIMIT = 56 * 1024 * 1024


def _swa_slot(hq):
    return (hq % G_B) * H_BKV + hq // G_B


def _slot_order(w, axis, base):
    piece = lambda lo, hi: lax.slice_in_dim(w, lo, hi, axis=axis)
    q = piece(base, base + HW)
    shape = q.shape[:axis] + (H_BKV, G_B, HEAD_DIM) + q.shape[axis + 1:]
    q = jnp.swapaxes(q.reshape(shape), axis, axis + 1).reshape(q.shape)
    return jnp.concatenate([piece(0, base), q, piece(base + HW, w.shape[axis])], axis=axis)


def _params(sem):
    return pltpu.CompilerParams(dimension_semantics=sem, vmem_limit_bytes=VMEM_LIMIT)


def _rms(x, g):
    return x * lax.rsqrt(jnp.mean(x * x, axis=-1, keepdims=True) + NORM_EPS) * g


def _dot(a, b):
    return jnp.dot(a, b, preferred_element_type=F32)


def _dot_nt(a, b):
    return lax.dot_general(a, b, (((1,), (1,)), ((), ())), preferred_element_type=F32)


def _dot_tn(a, b):
    return lax.dot_general(a, b, (((0,), (0,)), ((), ())), preferred_element_type=F32)


def _rope_tables(pos, dh):
    half = dh // 2
    inv = ROPE_THETA ** (-np.arange(half, dtype=np.float64) * (2.0 / dh))
    ang = np.asarray(pos, np.float64)[:, None] * inv[None, :]
    c, s = np.cos(ang), np.sin(ang)
    if dh == HEAD_DIM:
        cos = np.tile(np.concatenate([c, c], axis=1), (1, LANES // dh))
        sin = np.tile(np.concatenate([-s, s], axis=1), (1, LANES // dh))
    else:
        cos, sin = c, s
    return jnp.asarray(cos, F32), jnp.asarray(sin, F32)


def _rope_tables_t(pos):
    inv = ROPE_THETA ** (-np.arange(ROPE_HALF, dtype=np.float64) * (2.0 / HEAD_DIM))
    ang = inv[:, None] * np.asarray(pos, np.float64)[None, :]
    return jnp.asarray(np.cos(ang), F32), jnp.asarray(np.sin(ang), F32)


def _cache_rows_body(*refs, rope_heads):
    x_ref, g_ref, wt_ref, cos_ref, sin_ref = refs[:5]
    out_ref = refs[-1]
    h = _rms(x_ref[...], g_ref[...]).astype(BF16)
    zt = _dot_nt(wt_ref[...], h)
    cos, sin = cos_ref[...], sin_ref[...]
    for hh in range(rope_heads):
        r0 = hh * HEAD_DIM
        k1, k2 = zt[r0:r0 + ROPE_HALF], zt[r0 + ROPE_HALF:r0 + HEAD_DIM]
        out_ref[r0:r0 + ROPE_HALF, :] = k1 * cos - k2 * sin
        out_ref[r0 + ROPE_HALF:r0 + HEAD_DIM, :] = k2 * cos + k1 * sin
    out_ref[rope_heads * HEAD_DIM:, :] = zt[rope_heads * HEAD_DIM:]


def _cache_rows(x, gain, wt, cos_t, sin_t, buf, e, n_layers, nseq, seq, wb, tm):
    d = x.shape[1]
    cols = wt.shape[0]
    tm = min(tm, wb)
    assert wb % tm == 0 and seq % tm == 0 and (seq - wb) % tm == 0
    tiles, first = seq // tm, (seq - wb) // tm
    in_specs = [
        pl.BlockSpec((tm, d), lambda n, j: (n * tiles + first + j, 0)),
        pl.BlockSpec((1, d), lambda n, j: (0, 0)),
        pl.BlockSpec((cols, d), lambda n, j: (0, 0)),
        pl.BlockSpec((ROPE_HALF, tm), lambda n, j: (0, first + j)),
        pl.BlockSpec((ROPE_HALF, tm), lambda n, j: (0, first + j)),
    ]
    args = [x, gain.reshape(1, d), wt, cos_t, sin_t]
    if buf is not None:
        in_specs.append(pl.BlockSpec(memory_space=pl.ANY))
        args.append(buf)
    return pl.pallas_call(
        functools.partial(_cache_rows_body, rope_heads=cols // (2 * HEAD_DIM)),
        grid=(nseq, wb // tm),
        in_specs=in_specs,
        out_specs=pl.BlockSpec((None, None, cols, tm), lambda n, j: (e, n, 0, j)),
        out_shape=jax.ShapeDtypeStruct((n_layers, nseq, cols, wb), F32),
        input_output_aliases={5: 0} if buf is not None else {},
        compiler_params=_params(("parallel", "parallel")),
    )(*args)


def _rope64(z, cos, sin):
    w = z.shape[1]
    reps = w // LANES
    lane = lax.broadcasted_iota(jnp.int32, z.shape, 1)
    first_half = (lane & (HEAD_DIM - 1)) < ROPE_HALF
    partner = jnp.where(first_half, pltpu.roll(z, w - ROPE_HALF, 1), pltpu.roll(z, ROPE_HALF, 1))
    if reps > 1:
        cos = jnp.concatenate([cos] * reps, axis=1)
        sin = jnp.concatenate([sin] * reps, axis=1)
    return z * cos + partner * sin


def _mix_proj_body(x_ref, g_ref, w_ref, cos_ref, sin_ref, qa_ref, kv0_ref, kv1_ref, kv2_ref, zb_ref):
    h = _rms(x_ref[...], g_ref[...]).astype(BF16)
    cos, sin = cos_ref[...], sin_ref[...]
    kv_refs = (kv0_ref, kv1_ref, kv2_ref)
    for g in range(N_DIL):
        base = g * 3 * HW
        qa_ref[:, g * HW:(g + 1) * HW] = _rope64(_dot(h, w_ref[:, base:base + HW]), cos, sin)
        kv_refs[g][:, :HW] = _rope64(_dot(h, w_ref[:, base + HW:base + 2 * HW]), cos, sin)
        kv_refs[g][:, HW:] = _dot(h, w_ref[:, base + 2 * HW:base + 3 * HW])
    base = N_DIL * 3 * HW
    zb_ref[:, :HW] = _rope64(_dot(h, w_ref[:, base:base + HW]), cos, sin)
    zb_ref[:, HW:HW + KVB] = _rope64(_dot(h, w_ref[:, base + HW:base + HW + KVB]), cos, sin)
    zb_ref[:, HW + KVB:] = _dot(h, w_ref[:, base + HW + KVB:base + HW + 2 * KVB])


def _ret_proj_body(x_ref, g_ref, w_ref, cos_ref, sin_ref, z_ref):
    h = _rms(x_ref[...], g_ref[...]).astype(BF16)
    cos, sin = cos_ref[...], sin_ref[...]
    half = DK_R // 2
    for part, scale in ((0, 1.0), (1, DK_R ** -0.5)):
        for hh in range(H_R):
            c0 = part * H_R * DK_R + hh * DK_R
            z = _dot(h, w_ref[:, c0:c0 + DK_R])
            z1, z2 = z[:, :half], z[:, half:]
            z_ref[:, c0:c0 + half] = ((z1 * cos - z2 * sin) * scale).astype(z_ref.dtype)
            z_ref[:, c0 + half:c0 + DK_R] = ((z2 * cos + z1 * sin) * scale).astype(z_ref.dtype)
    c0 = 2 * H_R * DK_R
    for j in range(2 * H_R):
        cols = slice(c0 + j * DV_R, c0 + (j + 1) * DV_R)
        z_ref[:, cols] = _dot(h, w_ref[:, cols]).astype(z_ref.dtype)


def _norm_proj(body, x, gain, w, cos, sin, out_widths, tm, out_dtype=F32):
    m, d = x.shape
    tm = min(tm, m)
    assert m % tm == 0 and cos.shape[0] % tm == 0
    period = cos.shape[0] // tm
    n = w.shape[1]
    outs = pl.pallas_call(
        body,
        grid=(m // tm,),
        in_specs=[
            pl.BlockSpec((tm, d), lambda i: (i, 0)),
            pl.BlockSpec((1, d), lambda i: (0, 0)),
            pl.BlockSpec((d, n), lambda i: (0, 0), pipeline_mode=pl.Buffered(1)),
            pl.BlockSpec((tm, LANES), lambda i: (i % period, 0)),
            pl.BlockSpec((tm, LANES), lambda i: (i % period, 0)),
        ],
        out_specs=[pl.BlockSpec((tm, wd), lambda i: (i, 0)) for wd in out_widths],
        out_shape=[jax.ShapeDtypeStruct((m, wd), out_dtype) for wd in out_widths],
        compiler_params=_params(("parallel",)),
    )(x, gain.reshape(1, d), w, cos, sin)
    return outs


def _attn_pairs(blocks, mask, sinks):
    lane = lax.broadcasted_iota(jnp.int32, (BLOCK, LANES), 1)
    first = lane < HEAD_DIM
    owns = (first, jnp.logical_not(first))
    both = [(b, half) for b in range(len(blocks)) for half in range(len(owns))]
    kbs = [k.astype(BF16) for _, k, _ in blocks]
    vbs = [jnp.concatenate([v.astype(BF16), jnp.ones(v.shape, BF16)], axis=1) for _, _, v in blocks]
    s = [jnp.where(mask, _dot_nt(jnp.where(owns[half], blocks[b][0], 0.0).astype(BF16), kbs[b]), NEG_INF)
         for b, half in both]
    m = [jnp.max(x, axis=-1, keepdims=True) for x in s]
    if sinks is not None:
        m = [jnp.maximum(x, sinks[half]) for x, (_, half) in zip(m, both)]
    p = [jnp.exp(x - y).astype(BF16) for x, y in zip(s, m)]
    pv_l = [_dot(x, vbs[b]) for x, (b, _) in zip(p, both)]
    out = []
    for b in range(len(blocks)):
        i0, i1 = len(owns) * b, len(owns) * b + 1
        l0, l1 = pv_l[i0][:, LANES:], pv_l[i1][:, LANES:]
        if sinks is not None:
            l0 = l0 + jnp.exp(sinks[0] - m[i0])
            l1 = l1 + jnp.exp(sinks[1] - m[i1])
        out.append((jnp.where(first, m[i0], m[i1]), jnp.where(first, l0, l1),
                    jnp.where(first, pv_l[i0][:, :LANES], pv_l[i1][:, :LANES])))
    return out


def _prompt_attn_body(sink_ref, q0_ref, k0_ref, v0_ref, q1_ref, k1_ref, v1_ref, q2_ref, k2_ref, v2_ref,
                      qb_ref, kb_ref, vb_ref, oa_ref, ob_ref, acc_ref, m_ref, l_ref, *, seq):
    pair = pl.program_id(1)
    scale = HEAD_DIM ** -0.5
    row = lax.broadcasted_iota(jnp.int32, (BLOCK, 2 * BLOCK), 0)
    col = lax.broadcasted_iota(jnp.int32, (BLOCK, 2 * BLOCK), 1)
    causal = (lax.broadcasted_iota(jnp.int32, (BLOCK, BLOCK), 1)
              <= lax.broadcasted_iota(jnp.int32, (BLOCK, BLOCK), 0))

    def window_mask(min_back):
        return jnp.where(col < BLOCK, col - row - min_back, row - (col - BLOCK)) >= 0

    def rows(ref, start, n, stride):
        return ref[pl.ds(start, n, stride=stride), :] if stride > 1 else ref[pl.ds(start, n), :]

    def group_pass(q_ref, k_ref, v_ref, dil, min_back, sinks, emit):
        span = dil * BLOCK
        nblk = seq // span

        def run(starts, back, keys, mask):
            blocks = [(rows(q_ref, s, BLOCK, dil) * scale, rows(k_ref, s - back, keys, dil),
                       rows(v_ref, s - back, keys, dil)) for s in starts]
            for s, mlp in zip(starts, _attn_pairs(blocks, mask, sinks)):
                emit(s, mlp)

        for r0 in range(0, dil, FIRST_BLOCKS_IN_FLIGHT):
            run(list(range(r0, min(r0 + FIRST_BLOCKS_IN_FLIGHT, dil))), 0, BLOCK, causal)
        if nblk == 1:
            return
        mask = window_mask(min_back)
        per_iter = BLOCKS_IN_FLIGHT if (nblk - 1) % BLOCKS_IN_FLIGHT == 0 else 1
        trips = (nblk - 1) // per_iter

        def block_start(r, lb):
            return lb * span + r if isinstance(lb, int) else pl.multiple_of(lb * span, span) + r

        for r in range(dil):
            def body(it, carry, r=r):
                run([block_start(r, 1 + it * per_iter + j) for j in range(per_iter)], span, 2 * BLOCK, mask)
                return carry

            if trips == 1:
                body(0, 0)
            else:
                lax.fori_loop(0, trips, body, 0)

    def merge_emit(dil, last):
        def emit(start, mlp):
            m, l, pv = mlp
            idx = pl.ds(start, BLOCK, stride=dil)
            m_old = m_ref[idx, :]
            m_new = jnp.maximum(m_old, m)
            a = jnp.exp(m_old - m_new)
            b = jnp.exp(m - m_new)
            acc = a * acc_ref[idx, :] + b * pv
            den = a * l_ref[idx, :] + b * l
            if last:
                oa_ref[idx, :] = acc / den
            else:
                m_ref[idx, :] = m_new
                acc_ref[idx, :] = acc
                l_ref[idx, :] = den
        return emit

    def first_emit(start, mlp):
        idx = pl.ds(start, BLOCK)
        m_ref[idx, :], l_ref[idx, :], acc_ref[idx, :] = mlp

    groups = ((q0_ref, k0_ref, v0_ref), (q1_ref, k1_ref, v1_ref), (q2_ref, k2_ref, v2_ref))
    for g, (win, dil) in enumerate(DIL_PAIRS):
        emit = first_emit if g == 0 else merge_emit(dil, last=g == N_DIL - 1)
        group_pass(*groups[g], dil, BLOCK - win // dil, None, emit)

    sinks = [sink_ref[0, pair + G_B * half] for half in range(H_BKV)]

    def swa_emit(start, mlp):
        m, l, pv = mlp
        ob_ref[pl.ds(start, BLOCK), :] = pv / l

    group_pass(qb_ref, kb_ref, vb_ref, 1, BLOCK + 1 - SWA_WINDOW, sinks, swa_emit)


def _prompt_attn(qa, kvs, zb, sinks, nseq, seq):
    assert seq % (BLOCK * DIL_PAIRS[-1][1]) == 0 and KVB == LANES
    pairs = HW // LANES
    view = lambda a: a.reshape(nseq, seq, a.shape[1])
    col = lambda j: pl.BlockSpec((None, seq, LANES), lambda n, p: (n, 0, j(p)))
    in_specs = [pl.BlockSpec(memory_space=pltpu.SMEM)]
    args = [sinks.reshape(1, N_HEADS)]
    for g in range(N_DIL):
        in_specs += [col(lambda p, g=g: g * pairs + p), col(lambda p: p), col(lambda p: pairs + p)]
        args += [view(qa), view(kvs[g]), view(kvs[g])]
    in_specs += [col(lambda p: p), col(lambda p: pairs), col(lambda p: pairs + 1)]
    args += [view(zb)] * 3
    out_spec = pl.BlockSpec((None, seq, LANES), lambda n, p: (n, 0, p))
    oa, ob = pl.pallas_call(
        functools.partial(_prompt_attn_body, seq=seq),
        grid=(nseq, pairs),
        in_specs=in_specs,
        out_specs=[out_spec, out_spec],
        out_shape=[jax.ShapeDtypeStruct((nseq, seq, HW), F32)] * 2,
        scratch_shapes=[pltpu.VMEM((seq, LANES), F32)] * 3,
        compiler_params=_params(("parallel", "parallel")),
    )(*args)
    return oa.reshape(nseq * seq, HW), ob.reshape(nseq * seq, HW)


def _sample_attn_body(qa_ref, kn0_ref, kn1_ref, kn2_ref, zb_ref, c0_ref, c1_ref, c2_ref, cs_ref, sink_ref,
                      y_ref, pvn_ref, *, t_new, first):
    del first
    scale = HEAD_DIM ** -0.5
    r_i = lax.broadcasted_iota(jnp.int32, (N_HEADS * Q_PAD, HW), 0)
    l_i = lax.broadcasted_iota(jnp.int32, (N_HEADS * Q_PAD, HW), 1)
    own_head = (r_i >> Q_SHIFT) == (l_i >> HEAD_SHIFT)
    t_q = lax.broadcasted_iota(jnp.int32, (Q_PAD, 1), 0)
    new_key = lax.broadcasted_iota(jnp.int32, (Q_PAD, BLOCK), 1)

    def pad_rows(x, n):
        return jnp.concatenate([x, jnp.zeros((n - x.shape[0], x.shape[1]), x.dtype)], axis=0)

    def head_cols(ref, base, h):
        return ref[:, base + h * HEAD_DIM:base + (h + 1) * HEAD_DIM]

    kn_refs = (kn0_ref, kn1_ref, kn2_ref)
    c_refs = (c0_ref, c1_ref, c2_ref)
    stats = [[] for _ in range(N_HEADS)]
    for g, (win, dil) in enumerate(DIL_PAIRS):
        c_ref = c_refs[g]
        wb = c_ref.shape[-1]
        q_all = pad_rows(qa_ref[:, g * HW:(g + 1) * HW] * scale, Q_PAD)
        q_exp = jnp.where(own_head, jnp.concatenate([q_all] * N_HEADS, axis=0), 0.0).astype(BF16)
        s_new_all = _dot_nt(q_exp, pad_rows(kn_refs[g][:, :HW].astype(BF16), BLOCK))
        back = t_q - new_key
        mask_new = jnp.logical_and(jnp.logical_and(back >= 0, new_key < t_new), (back & (dil - 1)) == 0)
        ahead = lax.broadcasted_iota(jnp.int32, (Q_PAD, wb), 1) - t_q
        mask_cache = jnp.logical_and(ahead >= 0, (ahead & (dil - 1)) == 0)
        p_news = []
        partial = []
        for h in range(N_HEADS):
            qh = pad_rows(head_cols(qa_ref, g * HW, h) * scale, Q_PAD).astype(BF16)
            s_c = jnp.where(mask_cache, _dot(qh, c_ref[0, h].astype(BF16)), NEG_INF)
            s_n = jnp.where(mask_new, s_new_all[h * Q_PAD:(h + 1) * Q_PAD], NEG_INF)
            m = jnp.maximum(jnp.max(s_c, axis=-1, keepdims=True), jnp.max(s_n, axis=-1, keepdims=True))
            p_c = jnp.exp(s_c - m)
            p_n = jnp.exp(s_n - m)
            l = jnp.sum(p_c, axis=-1, keepdims=True) + jnp.sum(p_n, axis=-1, keepdims=True)
            partial.append((m, l, _dot_nt(p_c.astype(BF16), c_ref[1, h].astype(BF16))))
            p_news.append(p_n.astype(BF16))
        pvn_ref[...] = _dot(jnp.concatenate(p_news, axis=0), pad_rows(kn_refs[g][:, HW:].astype(BF16), BLOCK))
        for h, (m, l, pv) in enumerate(partial):
            pv = pv + pvn_ref[h * Q_PAD:(h + 1) * Q_PAD, h * HEAD_DIM:(h + 1) * HEAD_DIM]
            stats[h].append((m, l, pv))
    for h in range(N_HEADS):
        m_all = functools.reduce(jnp.maximum, [s[0] for s in stats[h]])
        acc = jnp.zeros((Q_PAD, HEAD_DIM), F32)
        den = jnp.zeros((Q_PAD, 1), F32)
        for m, l, pv in stats[h]:
            a = jnp.exp(m - m_all)
            acc = acc + a * pv
            den = den + a * l
        y_ref[:, h * HEAD_DIM:(h + 1) * HEAD_DIM] = (acc / den)[0:t_new]

    t_g = lax.broadcasted_iota(jnp.int32, (G_B * Q_PAD, 1), 0) & (Q_PAD - 1)
    key_g = lax.broadcasted_iota(jnp.int32, (G_B * Q_PAD, BLOCK), 1)
    mask_cache = key_g >= t_g + 1 + (BLOCK - SWA_WINDOW)
    mask_new = jnp.logical_and(t_g - key_g >= 0, key_g < t_new)
    for hk in range(H_BKV):
        qs = jnp.concatenate([pad_rows(head_cols(zb_ref, 0, _swa_slot(hk * G_B + gq)) * scale, Q_PAD)
                              for gq in range(G_B)], axis=0).astype(BF16)
        k_new = pad_rows(head_cols(zb_ref, HW, hk).astype(BF16), BLOCK)
        v_new = pad_rows(head_cols(zb_ref, HW + KVB, hk).astype(BF16), BLOCK)
        s_c = jnp.where(mask_cache, _dot(qs, cs_ref[0, hk].astype(BF16)), NEG_INF)
        s_n = jnp.where(mask_new, _dot_nt(qs, k_new), NEG_INF)
        sink = sink_ref[hk * G_B * Q_PAD:(hk + 1) * G_B * Q_PAD, 0:1]
        m = jnp.maximum(jnp.maximum(jnp.max(s_c, axis=-1, keepdims=True), jnp.max(s_n, axis=-1, keepdims=True)), sink)
        p_c = jnp.exp(s_c - m)
        p_n = jnp.exp(s_n - m)
        l = jnp.sum(p_c, axis=-1, keepdims=True) + jnp.sum(p_n, axis=-1, keepdims=True) + jnp.exp(sink - m)
        o = (_dot_nt(p_c.astype(BF16), cs_ref[1, hk].astype(BF16)) + _dot(p_n.astype(BF16), v_new)) / l
        for gq in range(G_B):
            slot = _swa_slot(hk * G_B + gq)
            y_ref[:, HW + slot * HEAD_DIM:HW + (slot + 1) * HEAD_DIM] = o[gq * Q_PAD:gq * Q_PAD + t_new]


def _run_side(side):
    d = side(lambda b: b)

    def body(*refs):
        d["body"](*refs, first=pl.program_id(0) == 0)

    return pl.pallas_call(
        body,
        grid=(d["steps"],),
        in_specs=d["in_specs"],
        out_specs=d["out_specs"],
        out_shape=d["out_shape"],
        input_output_aliases=d["aliases"],
        scratch_shapes=d["scratch"],
        compiler_params=_params(("arbitrary",)),
    )(*d["args"])


def _sample_attn(qa, kns, zb, caches, cache_swa, sinks, e, nreq, t_new):
    assert t_new <= Q_PAD
    qv = qa.reshape(nreq, t_new, N_DIL * HW)
    knv = [k.reshape(nreq, t_new, 2 * HW) for k in kns]
    zbv = zb.reshape(nreq, t_new, HW + 2 * KVB)
    to_native = lambda c: jnp.transpose(c, (0, 1, 3, 4, 5, 2))
    for (win, dil), c in zip(DIL_PAIRS, caches):
        assert c.shape[2] == win and win // dil == BLOCK
    assert cache_swa.shape[2] == BLOCK
    cv = [to_native(c) for c in caches] + [to_native(cache_swa)]
    sink_rows = jnp.broadcast_to(jnp.repeat(sinks, Q_PAD)[:, None], (N_HEADS * Q_PAD, LANES))

    def side(step):
        tok = lambda wd: pl.BlockSpec((None, t_new, wd), lambda *g: (step(*g), 0, 0))
        cache = lambda c: pl.BlockSpec((None, None) + c.shape[2:], lambda *g: (e, step(*g), 0, 0, 0, 0))
        return dict(
            body=functools.partial(_sample_attn_body, t_new=t_new),
            steps=nreq,
            in_specs=[tok(N_DIL * HW), tok(2 * HW), tok(2 * HW), tok(2 * HW), tok(HW + 2 * KVB)]
            + [cache(c) for c in cv] + [pl.BlockSpec((N_HEADS * Q_PAD, LANES), lambda *g: (0, 0))],
            args=[qv, *knv, zbv, *cv, sink_rows],
            out_specs=[tok(2 * HW)],
            out_shape=[jax.ShapeDtypeStruct((nreq, t_new, 2 * HW), F32)],
            aliases={},
            scratch=[pltpu.VMEM((N_HEADS * Q_PAD, HW), F32)],
        )

    return side


def _out_proj_body(*refs):
    *y_refs, x_ref, g_ref, w_ref, out_ref = refs
    mo = None
    r0 = 0
    for y_ref in y_refs:
        part = _dot(y_ref[...].astype(BF16), w_ref[r0:r0 + y_ref.shape[1], :])
        mo = part if mo is None else mo + part
        r0 += y_ref.shape[1]
    out_ref[...] = x_ref[...] + _rms(mo, g_ref[...])


def _out_proj(ys, x, gain, w, tm):
    m, d = x.shape
    tm = min(tm, m)
    assert m % tm == 0 and sum(y.shape[1] for y in ys) == w.shape[0]
    row = lambda wd: pl.BlockSpec((tm, wd), lambda i: (i, 0))
    return pl.pallas_call(
        _out_proj_body,
        grid=(m // tm,),
        in_specs=[row(y.shape[1]) for y in ys] + [
            row(d),
            pl.BlockSpec((1, d), lambda i: (0, 0)),
            pl.BlockSpec(w.shape, lambda i: (0, 0), pipeline_mode=pl.Buffered(1)),
        ],
        out_specs=row(d),
        out_shape=jax.ShapeDtypeStruct((m, d), F32),
        compiler_params=_params(("parallel",)),
    )(*ys, x, gain.reshape(1, d), w)


def _mlp_body(x_ref, g_pre_ref, g_post_ref, w1_ref, w2_ref, out_ref, h_ref, acc_ref):
    j = pl.program_id(1)

    @pl.when(j == 0)
    def _():
        h_ref[...] = _rms(x_ref[...], g_pre_ref[...]).astype(BF16)
        acc_ref[...] = jnp.zeros_like(acc_ref)

    a = jnp.maximum(_dot(h_ref[...], w1_ref[...]), 0.0)
    acc_ref[...] += _dot((a * a).astype(BF16), w2_ref[...])

    @pl.when(j == pl.num_programs(1) - 1)
    def _():
        out_ref[...] = x_ref[...] + _rms(acc_ref[...], g_post_ref[...])


def _mlp_and_side_body(*refs, n_in, n_out, side_body):
    mlp_in, side_in = refs[:N_MLP_IN], refs[N_MLP_IN:N_MLP_IN + n_in]
    out_ref = refs[N_MLP_IN + n_in]
    side_out = refs[N_MLP_IN + n_in + 1:N_MLP_IN + n_in + 1 + n_out]
    h_ref, acc_ref, *side_scratch = refs[N_MLP_IN + n_in + 1 + n_out:]
    _mlp_body(*mlp_in, out_ref, h_ref, acc_ref)
    first = jnp.logical_and(pl.program_id(0) == 0, pl.program_id(1) == 0)
    side_body(*side_in, *side_out, *side_scratch, first=first)


def _mlp(x, g_pre, g_post, w1, w2, tm, tf, side=None):
    m, d = x.shape
    ff = w1.shape[1]
    tm = min(tm, m)
    assert m % tm == 0 and ff % tf == 0
    grid = (m // tm, ff // tf)
    in_specs = [
        pl.BlockSpec((tm, d), lambda i, j: (i, 0)),
        pl.BlockSpec((1, d), lambda i, j: (0, 0)),
        pl.BlockSpec((1, d), lambda i, j: (0, 0)),
        pl.BlockSpec((d, tf), lambda i, j: (0, j)),
        pl.BlockSpec((tf, d), lambda i, j: (j, 0)),
    ]
    args = [x, g_pre.reshape(1, d), g_post.reshape(1, d), w1, w2]
    assert len(args) == N_MLP_IN
    out_spec = pl.BlockSpec((tm, d), lambda i, j: (i, 0))
    out_shape = jax.ShapeDtypeStruct((m, d), F32)
    scratch = [pltpu.VMEM((tm, d), BF16), pltpu.VMEM((tm, d), F32)]
    if side is None:
        return pl.pallas_call(
            _mlp_body, grid=grid, in_specs=in_specs, out_specs=out_spec, out_shape=out_shape,
            scratch_shapes=scratch, compiler_params=_params(("parallel", "arbitrary")),
        )(*args)
    s = side(lambda i, j: i * grid[1] + j)
    if s["steps"] != grid[0] * grid[1]:
        return _mlp(x, g_pre, g_post, w1, w2, tm, tf), _run_side(side)
    outs = pl.pallas_call(
        functools.partial(_mlp_and_side_body, n_in=len(s["in_specs"]), n_out=len(s["out_specs"]),
                          side_body=s["body"]),
        grid=grid,
        in_specs=in_specs + s["in_specs"],
        out_specs=[out_spec] + s["out_specs"],
        out_shape=[out_shape] + s["out_shape"],
        input_output_aliases={N_MLP_IN + i: 1 + o for i, o in s["aliases"].items()},
        scratch_shapes=scratch + s["scratch"],
        compiler_params=_params(("arbitrary", "arbitrary")),
    )(*args, *s["args"])
    return outs[0], outs[1:]


def _log_gammas():
    return jnp.asarray(np.log(1.0 - 2.0 ** (-5.0 - np.arange(H_R, dtype=np.float64))), F32)


def _group_norm_gate(o, gate):
    mu = jnp.mean(o, axis=-1, keepdims=True)
    var = jnp.mean(jnp.square(o - mu), axis=-1, keepdims=True)
    o = (o - mu) * lax.rsqrt(var + GN_EPS)
    return gate / (1.0 + jnp.exp(-gate)) * o


def _ret_prompt_body(lg_ref, q_ref, k_ref, v_ref, gate_ref, y_ref, st_ref, s_ref):
    c = pl.program_id(1)

    @pl.when(c == 0)
    def _():
        s_ref[...] = jnp.zeros_like(s_ref)

    chunk = q_ref.shape[0]
    row = lax.broadcasted_iota(jnp.int32, (chunk, chunk), 0)
    col = lax.broadcasted_iota(jnp.int32, (chunk, chunk), 1)
    diff = (row - col).astype(F32)
    ii = lax.broadcasted_iota(jnp.int32, (chunk, 1), 0).astype(F32)
    for h in range(H_R):
        lg = lg_ref[h]
        decay = jnp.where(diff >= 0, jnp.exp(jnp.maximum(diff, 0.0) * lg), 0.0)
        q = q_ref[:, h * DK_R:(h + 1) * DK_R]
        k = k_ref[:, h * DK_R:(h + 1) * DK_R]
        v = v_ref[:, h * DV_R:(h + 1) * DV_R].astype(BF16)
        a = _dot_nt(q.astype(BF16), k.astype(BF16)) * decay
        s_prev = s_ref[h]
        o = _dot(a.astype(BF16), v) + _dot((q * jnp.exp((ii + 1.0) * lg)).astype(BF16), s_prev.astype(BF16))
        kd = (k * jnp.exp((chunk - 1.0 - ii) * lg)).astype(BF16)
        s_ref[h] = jnp.exp(jnp.full((1, 1), chunk, F32) * lg) * s_prev + _dot_tn(kd, v)
        gate = gate_ref[:, h * DV_R:(h + 1) * DV_R].astype(F32)
        y_ref[:, h * DV_R:(h + 1) * DV_R] = _group_norm_gate(o, gate).astype(y_ref.dtype)

    @pl.when(c == pl.num_programs(1) - 1)
    def _():
        st_ref[...] = s_ref[...]


def _ret_prompt(zr, nseq, seq):
    nc = seq // RET_CHUNK
    zv = zr.reshape(nseq, seq, zr.shape[1])
    qk_w, v_w = H_R * DK_R, H_R * DV_R
    assert v_w == 2 * qk_w
    y, st = pl.pallas_call(
        _ret_prompt_body,
        grid=(nseq, nc),
        in_specs=[
            pl.BlockSpec(memory_space=pltpu.SMEM),
            pl.BlockSpec((None, RET_CHUNK, qk_w), lambda n, c: (n, c, 0)),
            pl.BlockSpec((None, RET_CHUNK, qk_w), lambda n, c: (n, c, 1)),
            pl.BlockSpec((None, RET_CHUNK, v_w), lambda n, c: (n, c, 1)),
            pl.BlockSpec((None, RET_CHUNK, v_w), lambda n, c: (n, c, 2)),
        ],
        out_specs=[
            pl.BlockSpec((None, RET_CHUNK, v_w), lambda n, c: (n, c, 0)),
            pl.BlockSpec((None, H_R, DK_R, DV_R), lambda n, c: (n, 0, 0, 0)),
        ],
        out_shape=[
            jax.ShapeDtypeStruct((nseq, seq, v_w), BF16),
            jax.ShapeDtypeStruct((nseq, H_R, DK_R, DV_R), F32),
        ],
        scratch_shapes=[pltpu.VMEM((H_R, DK_R, DV_R), F32)],
        compiler_params=_params(("parallel", "arbitrary")),
    )(_log_gammas(), zv, zv, zv, zv)
    return y.reshape(nseq * seq, v_w), st


def _ret_sample_body(*refs, t_new, has_prev, first):
    lg_ref, z_ref, s_in_ref = refs[:3]
    y_ref, s_out_ref, kpad_ref, vpad_ref = refs[4:] if has_prev else refs[3:]

    @pl.when(first)
    def _():
        kpad_ref[...] = jnp.zeros_like(kpad_ref)
        vpad_ref[...] = jnp.zeros_like(vpad_ref)

    rows = 8
    row = lax.broadcasted_iota(jnp.int32, (rows, BLOCK), 0)
    col = lax.broadcasted_iota(jnp.int32, (rows, BLOCK), 1)
    diff = (row - col).astype(F32)
    ii = lax.broadcasted_iota(jnp.int32, (rows, 1), 0).astype(F32)
    ik = lax.broadcasted_iota(jnp.int32, (t_new, 1), 0).astype(F32)
    zq = jnp.zeros((rows - t_new, DK_R), F32)
    for b, h in [(b, h) for b in range(z_ref.shape[0]) for h in range(H_R)]:
        lg = lg_ref[h]
        q = jnp.concatenate([z_ref[b, :, h * DK_R:(h + 1) * DK_R], zq], axis=0)
        k = z_ref[b, :, (H_R + h) * DK_R:(H_R + h + 1) * DK_R]
        v = z_ref[b, :, 2 * H_R * DK_R + h * DV_R:2 * H_R * DK_R + (h + 1) * DV_R]
        gate = z_ref[b, :, 2 * H_R * DK_R + (H_R + h) * DV_R:2 * H_R * DK_R + (H_R + h + 1) * DV_R]
        kpad_ref[0:t_new, :] = k
        vpad_ref[0:t_new, :] = v
        decay = jnp.where(diff >= 0, jnp.exp(jnp.maximum(diff, 0.0) * lg), 0.0)
        a = _dot_nt(q.astype(BF16), kpad_ref[...].astype(BF16)) * decay
        s_prev = s_in_ref[b, h]
        vb = vpad_ref[...].astype(BF16)
        o = _dot(a.astype(BF16), vb) + _dot((q * jnp.exp((ii + 1.0) * lg)).astype(BF16), s_prev.astype(BF16))
        kpad_ref[0:t_new, :] = k * jnp.exp((t_new - 1.0 - ik) * lg)
        upd = _dot_tn(kpad_ref[...].astype(BF16), vb)
        s_out_ref[b, h] = jnp.exp(jnp.full((1, 1), t_new, F32) * lg) * s_prev + upd
        y_ref[b, :, h * DV_R:(h + 1) * DV_R] = _group_norm_gate(o[0:t_new], gate)


def _ret_sample(zr, state, new_state, o, nreq, t_new, nb):
    zv = zr.reshape(nreq, t_new, zr.shape[1])
    has_prev = new_state is not None
    assert nreq % nb == 0

    def side(step):
        state_blk = pl.BlockSpec((None, nb, H_R, DK_R, DV_R), lambda *g: (o, step(*g), 0, 0, 0))
        in_specs = [
            pl.BlockSpec(memory_space=pltpu.SMEM),
            pl.BlockSpec((nb, t_new, zr.shape[1]), lambda *g: (step(*g), 0, 0)),
            state_blk,
        ]
        args = [_log_gammas(), zv, state]
        if has_prev:
            in_specs.append(pl.BlockSpec(memory_space=pl.ANY))
            args.append(new_state)
        return dict(
            body=functools.partial(_ret_sample_body, t_new=t_new, has_prev=has_prev),
            steps=nreq // nb,
            in_specs=in_specs,
            args=args,
            out_specs=[pl.BlockSpec((nb, t_new, H_R * DV_R), lambda *g: (step(*g), 0, 0)), state_blk],
            out_shape=[jax.ShapeDtypeStruct((nreq, t_new, H_R * DV_R), F32), jax.ShapeDtypeStruct(state.shape, F32)],
            aliases={3: 1} if has_prev else {},
            scratch=[pltpu.VMEM((BLOCK, DK_R), F32), pltpu.VMEM((BLOCK, DV_R), F32)],
        )

    return side


def kernel(x_prompt, x_sample, cache_dil_w128, cache_dil_w512, cache_dil_w2048, cache_swa, state_ret,
           norm_gains, w_in_mix, w_out_mix, attn_sinks, w_in_ret, w_out_ret, w_ff1, w_ff2):
    nseq, seq, d = x_prompt.shape
    nreq, t_new, _ = x_sample.shape
    depth = norm_gains.shape[0]
    assert seq % (BLOCK * DIL_PAIRS[-1][1]) == 0 and t_new <= DIL_PAIRS[1][1]
    xp = x_prompt.reshape(nseq * seq, d)
    xs = x_sample.reshape(nreq * t_new, d)
    dil_caches = (cache_dil_w128, cache_dil_w512, cache_dil_w2048)
    pos_p = np.arange(seq)
    pos_s = PAST_LEN + np.arange(t_new)
    rows_s = min(nreq * t_new, 512)
    rope_p = {dh: _rope_tables(pos_p, dh) for dh in (HEAD_DIM, DK_R)}
    rope_s = {dh: _rope_tables(np.tile(pos_s, rows_s // t_new), dh) for dh in (HEAD_DIM, DK_R)}
    mix_widths = (N_DIL * HW, 2 * HW, 2 * HW, 2 * HW, HW + 2 * KVB)

    rope_t = _rope_tables_t(pos_p)
    n_even = (depth + 1) // 2
    dil_p = [None for _ in DIL_PAIRS]
    swa_p = None
    dil_s = [[] for _ in DIL_PAIRS]
    swa_s, ret_p = [], []
    ret_s = None
    for layer in range(depth):
        gn = norm_gains[layer]
        if layer % 2 == 0:
            e = layer // 2
            w_in = _slot_order(w_in_mix[e], 1, N_DIL * 3 * HW).astype(BF16)
            w_out = _slot_order(w_out_mix[e], 0, HW).astype(BF16)
            qa_p, *kv_p, zb_p = _norm_proj(_mix_proj_body, xp, gn[0], w_in, *rope_p[HEAD_DIM], mix_widths, 256)
            qa_s, *kv_s, zb_s = _norm_proj(_mix_proj_body, xs, gn[0], w_in, *rope_s[HEAD_DIM], mix_widths, rows_s)
            kv_cols = lambda lo, hi: w_in_mix[e][:, lo:hi].T.astype(BF16)
            for g, (win, _) in enumerate(DIL_PAIRS):
                wt = kv_cols(g * 3 * HW + HW, (g + 1) * 3 * HW)
                dil_p[g] = _cache_rows(xp, gn[0], wt, *rope_t, dil_p[g], e, n_even, nseq, seq, min(win, seq), 512)
            wt = kv_cols(N_DIL * 3 * HW + HW, N_DIL * 3 * HW + HW + 2 * KVB)
            swa_p = _cache_rows(xp, gn[0], wt, *rope_t, swa_p, e, n_even, nseq, seq, min(SWA_WINDOW, seq), 512)
            oa_p, ob_p = _prompt_attn(qa_p, kv_p, zb_p, attn_sinks[e], nseq, seq)
            xp = _out_proj([oa_p, ob_p], xp, gn[1], w_out, 512)
            side = _sample_attn(qa_s, kv_s, zb_s, dil_caches, cache_swa, attn_sinks[e], e, nreq, t_new)
            for g in range(N_DIL):
                dil_s[g].append(kv_s[g].reshape(nreq, t_new, 2, N_HEADS, HEAD_DIM))
            swa_s.append(zb_s[:, HW:].reshape(nreq, t_new, 2, H_BKV, HEAD_DIM))
        else:
            o = layer // 2
            w_in = w_in_ret[o].astype(BF16)
            w_out = w_out_ret[o].astype(BF16)
            width = (w_in.shape[1],)
            zr_p, = _norm_proj(_ret_proj_body, xp, gn[0], w_in, *rope_p[DK_R], width, 256, out_dtype=BF16)
            zr_s, = _norm_proj(_ret_proj_body, xs, gn[0], w_in, *rope_s[DK_R], width, rows_s)
            y_p, st_p = _ret_prompt(zr_p, nseq, seq)
            side = _ret_sample(zr_s, state_ret, ret_s, o, nreq, t_new, 1)
            xp = _out_proj([y_p], xp, gn[1], w_out, 512)
            ret_p.append(st_p)
        w1 = w_ff1[layer].astype(BF16)
        w2 = w_ff2[layer].astype(BF16)
        xp, side_out = _mlp(xp, gn[2], gn[3], w1, w2, 1024, 512, side)
        if layer % 2 == 1:
            ret_s = side_out[1]
        y_s = side_out[0].reshape(nreq * t_new, side_out[0].shape[-1])
        xs = _out_proj([y_s], xs, gn[1], w_out, 512)
        xs = _mlp(xs, gn[2], gn[3], w1, w2, 512, 512)
    def rows_out(buf, heads):
        n_l, n_s, _, wb = buf.shape
        return jnp.transpose(buf.reshape(n_l, n_s, 2, heads, HEAD_DIM, wb), (0, 1, 5, 2, 3, 4))

    return (xp.reshape(nseq, seq, d), xs.reshape(nreq, t_new, d),
            rows_out(dil_p[0], N_HEADS), jnp.stack(dil_s[0]),
            rows_out(dil_p[1], N_HEADS), jnp.stack(dil_s[1]),
            rows_out(dil_p[2], N_HEADS), jnp.stack(dil_s[2]),
            rows_out(swa_p, H_BKV), jnp.stack(swa_s),
            jnp.stack(ret_p), ret_s)
```

```python
import functools

import numpy as np
import jax
import jax.numpy as jnp
from jax import lax
from jax.experimental import pallas as pl
from jax.experimental.pallas import tpu as pltpu

F32 = jnp.float32
BF16 = jnp.bfloat16

HEAD_DIM = 64
ROPE_HALF = HEAD_DIM // 2
N_HEADS = 8
HW = N_HEADS * HEAD_DIM
DIL_PAIRS = ((128, 1), (512, 4), (2048, 16))
N_DIL = len(DIL_PAIRS)
H_BKV = 2
G_B = N_HEADS // H_BKV
KVB = H_BKV * HEAD_DIM
SWA_WINDOW = 128
H_R = 4
DK_R = 256
DV_R = 512
RET_CHUNK = 128
BLOCK = 128
PAST_LEN = 8192
ROPE_THETA = 10000.0
NORM_EPS = 1e-6
GN_EPS = 1e-5
NEG_INF = -1e30
LANES = 128
BLOCKS_IN_FLIGHT = 3
FIRST_BLOCKS_IN_FLIGHT = 4
N_MLP_IN = 5
TM_PROJ = 512
TM_OUT_PROJ = 512
TM_CACHE_ROWS = 512
TM_MLP, TF_MLP = 1024, 512
TM_SAMPLE = 512
Q_PAD = 8
Q_SHIFT = Q_PAD.bit_length() - 1
HEAD_SHIFT = HEAD_DIM.bit_length() - 1
VMEM_LIMIT = 56 * 1024 * 1024


def _swa_slot(hq):
    return (hq % G_B) * H_BKV + hq // G_B


def _slot_order(w, axis, base):
    piece = lambda lo, hi: lax.slice_in_dim(w, lo, hi, axis=axis)
    q = piece(base, base + HW)
    shape = q.shape[:axis] + (H_BKV, G_B, HEAD_DIM) + q.shape[axis + 1:]
    q = jnp.swapaxes(q.reshape(shape), axis, axis + 1).reshape(q.shape)
    return jnp.concatenate([piece(0, base), q, piece(base + HW, w.shape[axis])], axis=axis)


def _params(sem):
    return pltpu.CompilerParams(dimension_semantics=sem, vmem_limit_bytes=VMEM_LIMIT)


def _rms(x, g):
    return x * lax.rsqrt(jnp.mean(x * x, axis=-1, keepdims=True) + NORM_EPS) * g


def _dot(a, b):
    return jnp.dot(a, b, preferred_element_type=F32)


def _dot_nt(a, b):
    return lax.dot_general(a, b, (((1,), (1,)), ((), ())), preferred_element_type=F32)


def _dot_tn(a, b):
    return lax.dot_general(a, b, (((0,), (0,)), ((), ())), preferred_element_type=F32)


def _rope_tables(pos, dh):
    half = dh // 2
    inv = ROPE_THETA ** (-np.arange(half, dtype=np.float64) * (2.0 / dh))
    ang = np.asarray(pos, np.float64)[:, None] * inv[None, :]
    c, s = np.cos(ang), np.sin(ang)
    if dh == HEAD_DIM:
        cos = np.tile(np.concatenate([c, c], axis=1), (1, LANES // dh))
        sin = np.tile(np.concatenate([-s, s], axis=1), (1, LANES // dh))
    else:
        cos, sin = c, s
    return jnp.asarray(cos, F32), jnp.asarray(sin, F32)


def _rope_tables_t(pos):
    inv = ROPE_THETA ** (-np.arange(ROPE_HALF, dtype=np.float64) * (2.0 / HEAD_DIM))
    ang = inv[:, None] * np.asarray(pos, np.float64)[None, :]
    return jnp.asarray(np.cos(ang), F32), jnp.asarray(np.sin(ang), F32)


def _cache_rows_body(*refs, rope_heads):
    x_ref, g_ref, wt_ref, cos_ref, sin_ref = refs[:5]
    out_ref = refs[-1]
    h = _rms(x_ref[...], g_ref[...]).astype(BF16)
    zt = _dot_nt(wt_ref[...], h)
    cos, sin = cos_ref[...], sin_ref[...]
    for hh in range(rope_heads):
        r0 = hh * HEAD_DIM
        k1, k2 = zt[r0:r0 + ROPE_HALF], zt[r0 + ROPE_HALF:r0 + HEAD_DIM]
        out_ref[r0:r0 + ROPE_HALF, :] = k1 * cos - k2 * sin
        out_ref[r0 + ROPE_HALF:r0 + HEAD_DIM, :] = k2 * cos + k1 * sin
    out_ref[rope_heads * HEAD_DIM:, :] = zt[rope_heads * HEAD_DIM:]


def _cache_rows(x, gain, wt, cos_t, sin_t, buf, e, n_layers, nseq, seq, wb, tm):
    d = x.shape[1]
    cols = wt.shape[0]
    tm = min(tm, wb)
    assert wb % tm == 0 and seq % tm == 0 and (seq - wb) % tm == 0
    tiles, first = seq // tm, (seq - wb) // tm
    in_specs = [
        pl.BlockSpec((tm, d), lambda n, j: (n * tiles + first + j, 0)),
        pl.BlockSpec((1, d), lambda n, j: (0, 0)),
        pl.BlockSpec((cols, d), lambda n, j: (0, 0)),
        pl.BlockSpec((ROPE_HALF, tm), lambda n, j: (0, first + j)),
        pl.BlockSpec((ROPE_HALF, tm), lambda n, j: (0, first + j)),
    ]
    args = [x, gain.reshape(1, d), wt, cos_t, sin_t]
    if buf is not None:
        in_specs.append(pl.BlockSpec(memory_space=pl.ANY))
        args.append(buf)
    return pl.pallas_call(
        functools.partial(_cache_rows_body, rope_heads=cols // (2 * HEAD_DIM)),
        grid=(nseq, wb // tm),
        in_specs=in_specs,
        out_specs=pl.BlockSpec((None, None, cols, tm), lambda n, j: (e, n, 0, j)),
        out_shape=jax.ShapeDtypeStruct((n_layers, nseq, cols, wb), F32),
        input_output_aliases={5: 0} if buf is not None else {},
        compiler_params=_params(("parallel", "parallel")),
    )(*args)


def _rope64(z, cos, sin):
    w = z.shape[1]
    reps = w // LANES
    lane = lax.broadcasted_iota(jnp.int32, z.shape, 1)
    first_half = (lane & (HEAD_DIM - 1)) < ROPE_HALF
    partner = jnp.where(first_half, pltpu.roll(z, w - ROPE_HALF, 1), pltpu.roll(z, ROPE_HALF, 1))
    if reps > 1:
        cos = jnp.concatenate([cos] * reps, axis=1)
        sin = jnp.concatenate([sin] * reps, axis=1)
    return z * cos + partner * sin


def _mix_proj_body(x_ref, g_ref, w_ref, cos_ref, sin_ref, qa_ref, kv0_ref, kv1_ref, kv2_ref, zb_ref):
    h = _rms(x_ref[...], g_ref[...]).astype(BF16)
    cos, sin = cos_ref[...], sin_ref[...]
    kv_refs = (kv0_ref, kv1_ref, kv2_ref)
    for g in range(N_DIL):
        base = g * 3 * HW
        qa_ref[:, g * HW:(g + 1) * HW] = _rope64(_dot(h, w_ref[:, base:base + HW]), cos, sin)
        kv_refs[g][:, :HW] = _rope64(_dot(h, w_ref[:, base + HW:base + 2 * HW]), cos, sin)
        kv_refs[g][:, HW:] = _dot(h, w_ref[:, base + 2 * HW:base + 3 * HW])
    base = N_DIL * 3 * HW
    zb_ref[:, :HW] = _rope64(_dot(h, w_ref[:, base:base + HW]), cos, sin)
    zb_ref[:, HW:HW + KVB] = _rope64(_dot(h, w_ref[:, base + HW:base + HW + KVB]), cos, sin)
    zb_ref[:, HW + KVB:] = _dot(h, w_ref[:, base + HW + KVB:base + HW + 2 * KVB])


def _ret_proj_body(x_ref, g_ref, w_ref, cos_ref, sin_ref, z_ref):
    h = _rms(x_ref[...], g_ref[...]).astype(BF16)
    cos, sin = cos_ref[...], sin_ref[...]
    half = DK_R // 2
    for part, scale in ((0, 1.0), (1, DK_R ** -0.5)):
        for hh in range(H_R):
            c0 = part * H_R * DK_R + hh * DK_R
            z = _dot(h, w_ref[:, c0:c0 + DK_R])
            z1, z2 = z[:, :half], z[:, half:]
            z_ref[:, c0:c0 + half] = ((z1 * cos - z2 * sin) * scale).astype(z_ref.dtype)
            z_ref[:, c0 + half:c0 + DK_R] = ((z2 * cos + z1 * sin) * scale).astype(z_ref.dtype)
    c0 = 2 * H_R * DK_R
    for j in range(2 * H_R):
        cols = slice(c0 + j * DV_R, c0 + (j + 1) * DV_R)
        z_ref[:, cols] = _dot(h, w_ref[:, cols]).astype(z_ref.dtype)


def _norm_proj(body, x, gain, w, cos, sin, out_widths, tm, out_dtype=F32):
    m, d = x.shape
    tm = min(tm, m)
    assert m % tm == 0 and cos.shape[0] % tm == 0
    period = cos.shape[0] // tm
    n = w.shape[1]
    outs = pl.pallas_call(
        body,
        grid=(m // tm,),
        in_specs=[
            pl.BlockSpec((tm, d), lambda i: (i, 0)),
            pl.BlockSpec((1, d), lambda i: (0, 0)),
            pl.BlockSpec((d, n), lambda i: (0, 0), pipeline_mode=pl.Buffered(1)),
            pl.BlockSpec((tm, LANES), lambda i: (i % period, 0)),
            pl.BlockSpec((tm, LANES), lambda i: (i % period, 0)),
        ],
        out_specs=[pl.BlockSpec((tm, wd), lambda i: (i, 0)) for wd in out_widths],
        out_shape=[jax.ShapeDtypeStruct((m, wd), out_dtype) for wd in out_widths],
        compiler_params=_params(("parallel",)),
    )(x, gain.reshape(1, d), w, cos, sin)
    return outs


def _attn_pairs(blocks, mask, sinks):
    lane = lax.broadcasted_iota(jnp.int32, (BLOCK, LANES), 1)
    first = lane < HEAD_DIM
    owns = (first, jnp.logical_not(first))
    both = [(b, half) for b in range(len(blocks)) for half in range(len(owns))]
    kbs = [k.astype(BF16) for _, k, _ in blocks]
    vbs = [jnp.concatenate([v.astype(BF16), jnp.ones(v.shape, BF16)], axis=1) for _, _, v in blocks]
    s = [jnp.where(mask, _dot_nt(jnp.where(owns[half], blocks[b][0], 0.0).astype(BF16), kbs[b]), NEG_INF)
         for b, half in both]
    m = [jnp.max(x, axis=-1, keepdims=True) for x in s]
    if sinks is not None:
        m = [jnp.maximum(x, sinks[half]) for x, (_, half) in zip(m, both)]
    p = [jnp.exp(x - y).astype(BF16) for x, y in zip(s, m)]
    pv_l = [_dot(x, vbs[b]) for x, (b, _) in zip(p, both)]
    out = []
    for b in range(len(blocks)):
        i0, i1 = len(owns) * b, len(owns) * b + 1
        l0, l1 = pv_l[i0][:, LANES:], pv_l[i1][:, LANES:]
        if sinks is not None:
            l0 = l0 + jnp.exp(sinks[0] - m[i0])
            l1 = l1 + jnp.exp(sinks[1] - m[i1])
        out.append((jnp.where(first, m[i0], m[i1]), jnp.where(first, l0, l1),
                    jnp.where(first, pv_l[i0][:, :LANES], pv_l[i1][:, :LANES])))
    return out


def _prompt_attn_body(sink_ref, q0_ref, k0_ref, v0_ref, q1_ref, k1_ref, v1_ref, q2_ref, k2_ref, v2_ref,
                      qb_ref, kb_ref, vb_ref, oa_ref, ob_ref, acc_ref, m_ref, l_ref, *, seq):
    pair = pl.program_id(1)
    scale = HEAD_DIM ** -0.5
    row = lax.broadcasted_iota(jnp.int32, (BLOCK, 2 * BLOCK), 0)
    col = lax.broadcasted_iota(jnp.int32, (BLOCK, 2 * BLOCK), 1)
    causal = (lax.broadcasted_iota(jnp.int32, (BLOCK, BLOCK), 1)
              <= lax.broadcasted_iota(jnp.int32, (BLOCK, BLOCK), 0))

    def window_mask(min_back):
        return jnp.where(col < BLOCK, col - row - min_back, row - (col - BLOCK)) >= 0

    def rows(ref, start, n, stride):
        return ref[pl.ds(start, n, stride=stride), :] if stride > 1 else ref[pl.ds(start, n), :]

    def group_pass(q_ref, k_ref, v_ref, dil, min_back, sinks, emit):
        span = dil * BLOCK
        nblk = seq // span

        def run(starts, back, keys, mask):
            blocks = [(rows(q_ref, s, BLOCK, dil) * scale, rows(k_ref, s - back, keys, dil),
                       rows(v_ref, s - back, keys, dil)) for s in starts]
            for s, mlp in zip(starts, _attn_pairs(blocks, mask, sinks)):
                emit(s, mlp)

        for r0 in range(0, dil, FIRST_BLOCKS_IN_FLIGHT):
            run(list(range(r0, min(r0 + FIRST_BLOCKS_IN_FLIGHT, dil))), 0, BLOCK, causal)
        if nblk == 1:
            return
        mask = window_mask(min_back)
        per_iter = BLOCKS_IN_FLIGHT if (nblk - 1) % BLOCKS_IN_FLIGHT == 0 else 1
        trips = (nblk - 1) // per_iter

        def block_start(r, lb):
            return lb * span + r if isinstance(lb, int) else pl.multiple_of(lb * span, span) + r

        for r in range(dil):
            def body(it, carry, r=r):
                run([block_start(r, 1 + it * per_iter + j) for j in range(per_iter)], span, 2 * BLOCK, mask)
                return carry

            if trips == 1:
                body(0, 0)
            else:
                lax.fori_loop(0, trips, body, 0)

    def merge_emit(dil, last):
        def emit(start, mlp):
            m, l, pv = mlp
            idx = pl.ds(start, BLOCK, stride=dil)
            m_old = m_ref[idx, :]
            m_new = jnp.maximum(m_old, m)
            a = jnp.exp(m_old - m_new)
            b = jnp.exp(m - m_new)
            acc = a * acc_ref[idx, :] + b * pv
            den = a * l_ref[idx, :] + b * l
            if last:
                oa_ref[idx, :] = acc / den
            else:
                m_ref[idx, :] = m_new
                acc_ref[idx, :] = acc
                l_ref[idx, :] = den
        return emit

    def first_emit(start, mlp):
        idx = pl.ds(start, BLOCK)
        m_ref[idx, :], l_ref[idx, :], acc_ref[idx, :] = mlp

    groups = ((q0_ref, k0_ref, v0_ref), (q1_ref, k1_ref, v1_ref), (q2_ref, k2_ref, v2_ref))
    for g, (win, dil) in enumerate(DIL_PAIRS):
        emit = first_emit if g == 0 else merge_emit(dil, last=g == N_DIL - 1)
        group_pass(*groups[g], dil, BLOCK - win // dil, None, emit)

    sinks = [sink_ref[0, pair + G_B * half] for half in range(H_BKV)]

    def swa_emit(start, mlp):
        m, l, pv = mlp
        ob_ref[pl.ds(start, BLOCK), :] = pv / l

    group_pass(qb_ref, kb_ref, vb_ref, 1, BLOCK + 1 - SWA_WINDOW, sinks, swa_emit)


def _prompt_attn(qa, kvs, zb, sinks, nseq, seq):
    assert seq % (BLOCK * DIL_PAIRS[-1][1]) == 0 and KVB == LANES
    pairs = HW // LANES
    view = lambda a: a.reshape(nseq, seq, a.shape[1])
    col = lambda j: pl.BlockSpec((None, seq, LANES), lambda n, p: (n, 0, j(p)))
    in_specs = [pl.BlockSpec(memory_space=pltpu.SMEM)]
    args = [sinks.reshape(1, N_HEADS)]
    for g in range(N_DIL):
        in_specs += [col(lambda p, g=g: g * pairs + p), col(lambda p: p), col(lambda p: pairs + p)]
        args += [view(qa), view(kvs[g]), view(kvs[g])]
    in_specs += [col(lambda p: p), col(lambda p: pairs), col(lambda p: pairs + 1)]
    args += [view(zb)] * 3
    out_spec = pl.BlockSpec((None, seq, LANES), lambda n, p: (n, 0, p))
    oa, ob = pl.pallas_call(
        functools.partial(_prompt_attn_body, seq=seq),
        grid=(nseq, pairs),
        in_specs=in_specs,
        out_specs=[out_spec, out_spec],
        out_shape=[jax.ShapeDtypeStruct((nseq, seq, HW), F32)] * 2,
        scratch_shapes=[pltpu.VMEM((seq, LANES), F32)] * 3,
        compiler_params=_params(("parallel", "parallel")),
    )(*args)
    return oa.reshape(nseq * seq, HW), ob.reshape(nseq * seq, HW)


def _sample_attn_body(qa_ref, kn0_ref, kn1_ref, kn2_ref, zb_ref, c0_ref, c1_ref, c2_ref, cs_ref, sink_ref,
                      y_ref, pvn_ref, *, t_new, first):
    del first
    scale = HEAD_DIM ** -0.5
    r_i = lax.broadcasted_iota(jnp.int32, (N_HEADS * Q_PAD, HW), 0)
    l_i = lax.broadcasted_iota(jnp.int32, (N_HEADS * Q_PAD, HW), 1)
    own_head = (r_i >> Q_SHIFT) == (l_i >> HEAD_SHIFT)
    t_q = lax.broadcasted_iota(jnp.int32, (Q_PAD, 1), 0)
    new_key = lax.broadcasted_iota(jnp.int32, (Q_PAD, BLOCK), 1)

    def pad_rows(x, n):
        return jnp.concatenate([x, jnp.zeros((n - x.shape[0], x.shape[1]), x.dtype)], axis=0)

    def head_cols(ref, base, h):
        return ref[:, base + h * HEAD_DIM:base + (h + 1) * HEAD_DIM]

    kn_refs = (kn0_ref, kn1_ref, kn2_ref)
    c_refs = (c0_ref, c1_ref, c2_ref)
    stats = [[] for _ in range(N_HEADS)]
    for g, (win, dil) in enumerate(DIL_PAIRS):
        c_ref = c_refs[g]
        wb = c_ref.shape[-1]
        q_all = pad_rows(qa_ref[:, g * HW:(g + 1) * HW] * scale, Q_PAD)
        q_exp = jnp.where(own_head, jnp.concatenate([q_all] * N_HEADS, axis=0), 0.0).astype(BF16)
        s_new_all = _dot_nt(q_exp, pad_rows(kn_refs[g][:, :HW].astype(BF16), BLOCK))
        back = t_q - new_key
        mask_new = jnp.logical_and(jnp.logical_and(back >= 0, new_key < t_new), (back & (dil - 1)) == 0)
        ahead = lax.broadcasted_iota(jnp.int32, (Q_PAD, wb), 1) - t_q
        mask_cache = jnp.logical_and(ahead >= 0, (ahead & (dil - 1)) == 0)
        p_news = []
        partial = []
        for h in range(N_HEADS):
            qh = pad_rows(head_cols(qa_ref, g * HW, h) * scale, Q_PAD).astype(BF16)
            s_c = jnp.where(mask_cache, _dot(qh, c_ref[0, h].astype(BF16)), NEG_INF)
            s_n = jnp.where(mask_new, s_new_all[h * Q_PAD:(h + 1) * Q_PAD], NEG_INF)
            m = jnp.maximum(jnp.max(s_c, axis=-1, keepdims=True), jnp.max(s_n, axis=-1, keepdims=True))
            p_c = jnp.exp(s_c - m)
            p_n = jnp.exp(s_n - m)
            l = jnp.sum(p_c, axis=-1, keepdims=True) + jnp.sum(p_n, axis=-1, keepdims=True)
            partial.append((m, l, _dot_nt(p_c.astype(BF16), c_ref[1, h].astype(BF16))))
            p_news.append(p_n.astype(BF16))
        pvn_ref[...] = _dot(jnp.concatenate(p_news, axis=0), pad_rows(kn_refs[g][:, HW:].astype(BF16), BLOCK))
        for h, (m, l, pv) in enumerate(partial):
            pv = pv + pvn_ref[h * Q_PAD:(h + 1) * Q_PAD, h * HEAD_DIM:(h + 1) * HEAD_DIM]
            stats[h].append((m, l, pv))
    for h in range(N_HEADS):
        m_all = functools.reduce(jnp.maximum, [s[0] for s in stats[h]])
        acc = jnp.zeros((Q_PAD, HEAD_DIM), F32)
        den = jnp.zeros((Q_PAD, 1), F32)
        for m, l, pv in stats[h]:
            a = jnp.exp(m - m_all)
            acc = acc + a * pv
            den = den + a * l
        y_ref[:, h * HEAD_DIM:(h + 1) * HEAD_DIM] = (acc / den)[0:t_new]

    t_g = lax.broadcasted_iota(jnp.int32, (G_B * Q_PAD, 1), 0) & (Q_PAD - 1)
    key_g = lax.broadcasted_iota(jnp.int32, (G_B * Q_PAD, BLOCK), 1)
    mask_cache = key_g >= t_g + 1 + (BLOCK - SWA_WINDOW)
    mask_new = jnp.logical_and(t_g - key_g >= 0, key_g < t_new)
    for hk in range(H_BKV):
        qs = jnp.concatenate([pad_rows(head_cols(zb_ref, 0, _swa_slot(hk * G_B + gq)) * scale, Q_PAD)
                              for gq in range(G_B)], axis=0).astype(BF16)
        k_new = pad_rows(head_cols(zb_ref, HW, hk).astype(BF16), BLOCK)
        v_new = pad_rows(head_cols(zb_ref, HW + KVB, hk).astype(BF16), BLOCK)
        s_c = jnp.where(mask_cache, _dot(qs, cs_ref[0, hk].astype(BF16)), NEG_INF)
        s_n = jnp.where(mask_new, _dot_nt(qs, k_new), NEG_INF)
        sink = sink_ref[hk * G_B * Q_PAD:(hk + 1) * G_B * Q_PAD, 0:1]
        m = jnp.maximum(jnp.maximum(jnp.max(s_c, axis=-1, keepdims=True), jnp.max(s_n, axis=-1, keepdims=True)), sink)
        p_c = jnp.exp(s_c - m)
        p_n = jnp.exp(s_n - m)
        l = jnp.sum(p_c, axis=-1, keepdims=True) + jnp.sum(p_n, axis=-1, keepdims=True) + jnp.exp(sink - m)
        o = (_dot_nt(p_c.astype(BF16), cs_ref[1, hk].astype(BF16)) + _dot(p_n.astype(BF16), v_new)) / l
        for gq in range(G_B):
            slot = _swa_slot(hk * G_B + gq)
            y_ref[:, HW + slot * HEAD_DIM:HW + (slot + 1) * HEAD_DIM] = o[gq * Q_PAD:gq * Q_PAD + t_new]


def _run_side(side):
    d = side(lambda b: b)

    def body(*refs):
        d["body"](*refs, first=pl.program_id(0) == 0)

    return pl.pallas_call(
        body,
        grid=(d["steps"],),
        in_specs=d["in_specs"],
        out_specs=d["out_specs"],
        out_shape=d["out_shape"],
        input_output_aliases=d["aliases"],
        scratch_shapes=d["scratch"],
        compiler_params=_params(("arbitrary",)),
    )(*d["args"])


def _sample_attn(qa, kns, zb, caches, cache_swa, sinks, e, nreq, t_new):
    assert t_new <= Q_PAD
    qv = qa.reshape(nreq, t_new, N_DIL * HW)
    knv = [k.reshape(nreq, t_new, 2 * HW) for k in kns]
    zbv = zb.reshape(nreq, t_new, HW + 2 * KVB)
    to_native = lambda c: jnp.transpose(c, (0, 1, 3, 4, 5, 2))
    for (win, dil), c in zip(DIL_PAIRS, caches):
        assert c.shape[2] == win and win // dil == BLOCK
    assert cache_swa.shape[2] == BLOCK
    cv = [to_native(c) for c in caches] + [to_native(cache_swa)]
    sink_rows = jnp.broadcast_to(jnp.repeat(sinks, Q_PAD)[:, None], (N_HEADS * Q_PAD, LANES))

    def side(step):
        tok = lambda wd: pl.BlockSpec((None, t_new, wd), lambda *g: (step(*g), 0, 0))
        cache = lambda c: pl.BlockSpec((None, None) + c.shape[2:], lambda *g: (e, step(*g), 0, 0, 0, 0))
        return dict(
            body=functools.partial(_sample_attn_body, t_new=t_new),
            steps=nreq,
            in_specs=[tok(N_DIL * HW), tok(2 * HW), tok(2 * HW), tok(2 * HW), tok(HW + 2 * KVB)]
            + [cache(c) for c in cv] + [pl.BlockSpec((N_HEADS * Q_PAD, LANES), lambda *g: (0, 0))],
            args=[qv, *knv, zbv, *cv, sink_rows],
            out_specs=[tok(2 * HW)],
            out_shape=[jax.ShapeDtypeStruct((nreq, t_new, 2 * HW), F32)],
            aliases={},
            scratch=[pltpu.VMEM((N_HEADS * Q_PAD, HW), F32)],
        )

    return side


def _out_proj_body(*refs):
    *y_refs, x_ref, g_ref, w_ref, out_ref = refs
    mo = None
    r0 = 0
    for y_ref in y_refs:
        part = _dot(y_ref[...].astype(BF16), w_ref[r0:r0 + y_ref.shape[1], :])
        mo = part if mo is None else mo + part
        r0 += y_ref.shape[1]
    out_ref[...] = x_ref[...] + _rms(mo, g_ref[...])


def _out_proj(ys, x, gain, w, tm):
    m, d = x.shape
    tm = min(tm, m)
    assert m % tm == 0 and sum(y.shape[1] for y in ys) == w.shape[0]
    row = lambda wd: pl.BlockSpec((tm, wd), lambda i: (i, 0))
    return pl.pallas_call(
        _out_proj_body,
        grid=(m // tm,),
        in_specs=[row(y.shape[1]) for y in ys] + [
            row(d),
            pl.BlockSpec((1, d), lambda i: (0, 0)),
            pl.BlockSpec(w.shape, lambda i: (0, 0), pipeline_mode=pl.Buffered(1)),
        ],
        out_specs=row(d),
        out_shape=jax.ShapeDtypeStruct((m, d), F32),
        compiler_params=_params(("parallel",)),
    )(*ys, x, gain.reshape(1, d), w)


def _mlp_body(x_ref, g_pre_ref, g_post_ref, w1_ref, w2_ref, out_ref, h_ref, acc_ref):
    j = pl.program_id(1)

    @pl.when(j == 0)
    def _():
        h_ref[...] = _rms(x_ref[...], g_pre_ref[...]).astype(BF16)
        acc_ref[...] = jnp.zeros_like(acc_ref)

    a = jnp.maximum(_dot(h_ref[...], w1_ref[...]), 0.0)
    acc_ref[...] += _dot((a * a).astype(BF16), w2_ref[...])

    @pl.when(j == pl.num_programs(1) - 1)
    def _():
        out_ref[...] = x_ref[...] + _rms(acc_ref[...], g_post_ref[...])


def _mlp_and_side_body(*refs, n_in, n_out, side_body):
    mlp_in, side_in = refs[:N_MLP_IN], refs[N_MLP_IN:N_MLP_IN + n_in]
    out_ref = refs[N_MLP_IN + n_in]
    side_out = refs[N_MLP_IN + n_in + 1:N_MLP_IN + n_in + 1 + n_out]
    h_ref, acc_ref, *side_scratch = refs[N_MLP_IN + n_in + 1 + n_out:]
    _mlp_body(*mlp_in, out_ref, h_ref, acc_ref)
    first = jnp.logical_and(pl.program_id(0) == 0, pl.program_id(1) == 0)
    side_body(*side_in, *side_out, *side_scratch, first=first)


def _mlp(x, g_pre, g_post, w1, w2, tm, tf, side=None):
    m, d = x.shape
    ff = w1.shape[1]
    tm = min(tm, m)
    assert m % tm == 0 and ff % tf == 0
    grid = (m // tm, ff // tf)
    in_specs = [
        pl.BlockSpec((tm, d), lambda i, j: (i, 0)),
        pl.BlockSpec((1, d), lambda i, j: (0, 0)),
        pl.BlockSpec((1, d), lambda i, j: (0, 0)),
        pl.BlockSpec((d, tf), lambda i, j: (0, j)),
        pl.BlockSpec((tf, d), lambda i, j: (j, 0)),
    ]
    args = [x, g_pre.reshape(1, d), g_post.reshape(1, d), w1, w2]
    assert len(args) == N_MLP_IN
    out_spec = pl.BlockSpec((tm, d), lambda i, j: (i, 0))
    out_shape = jax.ShapeDtypeStruct((m, d), F32)
    scratch = [pltpu.VMEM((tm, d), BF16), pltpu.VMEM((tm, d), F32)]
    if side is None:
        return pl.pallas_call(
            _mlp_body, grid=grid, in_specs=in_specs, out_specs=out_spec, out_shape=out_shape,
            scratch_shapes=scratch, compiler_params=_params(("parallel", "arbitrary")),
        )(*args)
    s = side(lambda i, j: i * grid[1] + j)
    if s["steps"] != grid[0] * grid[1]:
        return _mlp(x, g_pre, g_post, w1, w2, tm, tf), _run_side(side)
    outs = pl.pallas_call(
        functools.partial(_mlp_and_side_body, n_in=len(s["in_specs"]), n_out=len(s["out_specs"]),
                          side_body=s["body"]),
        grid=grid,
        in_specs=in_specs + s["in_specs"],
        out_specs=[out_spec] + s["out_specs"],
        out_shape=[out_shape] + s["out_shape"],
        input_output_aliases={N_MLP_IN + i: 1 + o for i, o in s["aliases"].items()},
        scratch_shapes=scratch + s["scratch"],
        compiler_params=_params(("arbitrary", "arbitrary")),
    )(*args, *s["args"])
    return outs[0], outs[1:]


def _log_gammas():
    return jnp.asarray(np.log(1.0 - 2.0 ** (-5.0 - np.arange(H_R, dtype=np.float64))), F32)


def _group_norm_gate(o, gate):
    mu = jnp.mean(o, axis=-1, keepdims=True)
    var = jnp.mean(jnp.square(o - mu), axis=-1, keepdims=True)
    o = (o - mu) * lax.rsqrt(var + GN_EPS)
    return gate / (1.0 + jnp.exp(-gate)) * o


def _ret_prompt_body(lg_ref, q_ref, k_ref, v_ref, gate_ref, y_ref, st_ref, s_ref):
    c = pl.program_id(1)

    @pl.when(c == 0)
    def _():
        s_ref[...] = jnp.zeros_like(s_ref)

    chunk = q_ref.shape[0]
    row = lax.broadcasted_iota(jnp.int32, (chunk, chunk), 0)
    col = lax.broadcasted_iota(jnp.int32, (chunk, chunk), 1)
    diff = (row - col).astype(F32)
    ii = lax.broadcasted_iota(jnp.int32, (chunk, 1), 0).astype(F32)
    for h in range(H_R):
        lg = lg_ref[h]
        decay = jnp.where(diff >= 0, jnp.exp(jnp.maximum(diff, 0.0) * lg), 0.0)
        q = q_ref[:, h * DK_R:(h + 1) * DK_R]
        k = k_ref[:, h * DK_R:(h + 1) * DK_R]
        v = v_ref[:, h * DV_R:(h + 1) * DV_R].astype(BF16)
        a = _dot_nt(q.astype(BF16), k.astype(BF16)) * decay
        s_prev = s_ref[h]
        o = _dot(a.astype(BF16), v) + _dot((q * jnp.exp((ii + 1.0) * lg)).astype(BF16), s_prev.astype(BF16))
        kd = (k * jnp.exp((chunk - 1.0 - ii) * lg)).astype(BF16)
        s_ref[h] = jnp.exp(jnp.full((1, 1), chunk, F32) * lg) * s_prev + _dot_tn(kd, v)
        gate = gate_ref[:, h * DV_R:(h + 1) * DV_R].astype(F32)
        y_ref[:, h * DV_R:(h + 1) * DV_R] = _group_norm_gate(o, gate).astype(y_ref.dtype)

    @pl.when(c == pl.num_programs(1) - 1)
    def _():
        st_ref[...] = s_ref[...]


def _ret_prompt(zr, nseq, seq):
    nc = seq // RET_CHUNK
    zv = zr.reshape(nseq, seq, zr.shape[1])
    qk_w, v_w = H_R * DK_R, H_R * DV_R
    assert v_w == 2 * qk_w
    y, st = pl.pallas_call(
        _ret_prompt_body,
        grid=(nseq, nc),
        in_specs=[
            pl.BlockSpec(memory_space=pltpu.SMEM),
            pl.BlockSpec((None, RET_CHUNK, qk_w), lambda n, c: (n, c, 0)),
            pl.BlockSpec((None, RET_CHUNK, qk_w), lambda n, c: (n, c, 1)),
            pl.BlockSpec((None, RET_CHUNK, v_w), lambda n, c: (n, c, 1)),
            pl.BlockSpec((None, RET_CHUNK, v_w), lambda n, c: (n, c, 2)),
        ],
        out_specs=[
            pl.BlockSpec((None, RET_CHUNK, v_w), lambda n, c: (n, c, 0)),
            pl.BlockSpec((None, H_R, DK_R, DV_R), lambda n, c: (n, 0, 0, 0)),
        ],
        out_shape=[
            jax.ShapeDtypeStruct((nseq, seq, v_w), BF16),
            jax.ShapeDtypeStruct((nseq, H_R, DK_R, DV_R), F32),
        ],
        scratch_shapes=[pltpu.VMEM((H_R, DK_R, DV_R), F32)],
        compiler_params=_params(("parallel", "arbitrary")),
    )(_log_gammas(), zv, zv, zv, zv)
    return y.reshape(nseq * seq, v_w), st


def _ret_sample_body(*refs, t_new, has_prev, first):
    lg_ref, z_ref, s_in_ref = refs[:3]
    y_ref, s_out_ref, kpad_ref, vpad_ref = refs[4:] if has_prev else refs[3:]

    @pl.when(first)
    def _():
        kpad_ref[...] = jnp.zeros_like(kpad_ref)
        vpad_ref[...] = jnp.zeros_like(vpad_ref)

    rows = Q_PAD
    row = lax.broadcasted_iota(jnp.int32, (rows, BLOCK), 0)
    col = lax.broadcasted_iota(jnp.int32, (rows, BLOCK), 1)
    diff = (row - col).astype(F32)
    ii = lax.broadcasted_iota(jnp.int32, (rows, 1), 0).astype(F32)
    ik = lax.broadcasted_iota(jnp.int32, (t_new, 1), 0).astype(F32)
    zq = jnp.zeros((rows - t_new, DK_R), F32)
    for b, h in [(b, h) for b in range(z_ref.shape[0]) for h in range(H_R)]:
        lg = lg_ref[h]
        q = jnp.concatenate([z_ref[b, :, h * DK_R:(h + 1) * DK_R], zq], axis=0)
        k = z_ref[b, :, (H_R + h) * DK_R:(H_R + h + 1) * DK_R]
        v = z_ref[b, :, 2 * H_R * DK_R + h * DV_R:2 * H_R * DK_R + (h + 1) * DV_R]
        gate = z_ref[b, :, 2 * H_R * DK_R + (H_R + h) * DV_R:2 * H_R * DK_R + (H_R + h + 1) * DV_R]
        kpad_ref[0:t_new, :] = k
        vpad_ref[0:t_new, :] = v
        decay = jnp.where(diff >= 0, jnp.exp(jnp.maximum(diff, 0.0) * lg), 0.0)
        a = _dot_nt(q.astype(BF16), kpad_ref[...].astype(BF16)) * decay
        s_prev = s_in_ref[b, h]
        vb = vpad_ref[...].astype(BF16)
        o = _dot(a.astype(BF16), vb) + _dot((q * jnp.exp((ii + 1.0) * lg)).astype(BF16), s_prev.astype(BF16))
        kpad_ref[0:t_new, :] = k * jnp.exp((t_new - 1.0 - ik) * lg)
        upd = _dot_tn(kpad_ref[...].astype(BF16), vb)
        s_out_ref[b, h] = jnp.exp(jnp.full((1, 1), t_new, F32) * lg) * s_prev + upd
        y_ref[b, :, h * DV_R:(h + 1) * DV_R] = _group_norm_gate(o[0:t_new], gate)


def _ret_sample(zr, state, new_state, o, nreq, t_new, nb):
    zv = zr.reshape(nreq, t_new, zr.shape[1])
    has_prev = new_state is not None
    assert nreq % nb == 0

    def side(step):
        state_blk = pl.BlockSpec((None, nb, H_R, DK_R, DV_R), lambda *g: (o, step(*g), 0, 0, 0))
        in_specs = [
            pl.BlockSpec(memory_space=pltpu.SMEM),
            pl.BlockSpec((nb, t_new, zr.shape[1]), lambda *g: (step(*g), 0, 0)),
            state_blk,
        ]
        args = [_log_gammas(), zv, state]
        if has_prev:
            in_specs.append(pl.BlockSpec(memory_space=pl.ANY))
            args.append(new_state)
        return dict(
            body=functools.partial(_ret_sample_body, t_new=t_new, has_prev=has_prev),
            steps=nreq // nb,
            in_specs=in_specs,
            args=args,
            out_specs=[pl.BlockSpec((nb, t_new, H_R * DV_R), lambda *g: (step(*g), 0, 0)), state_blk],
            out_shape=[jax.ShapeDtypeStruct((nreq, t_new, H_R * DV_R), F32), jax.ShapeDtypeStruct(state.shape, F32)],
            aliases={3: 1} if has_prev else {},
            scratch=[pltpu.VMEM((BLOCK, DK_R), F32), pltpu.VMEM((BLOCK, DV_R), F32)],
        )

    return side


def kernel(x_prompt, x_sample, cache_dil_w128, cache_dil_w512, cache_dil_w2048, cache_swa, state_ret,
           norm_gains, w_in_mix, w_out_mix, attn_sinks, w_in_ret, w_out_ret, w_ff1, w_ff2):
    nseq, seq, d = x_prompt.shape
    nreq, t_new, _ = x_sample.shape
    depth = norm_gains.shape[0]
    assert seq % (BLOCK * DIL_PAIRS[-1][1]) == 0 and t_new <= DIL_PAIRS[1][1]
    xp = x_prompt.reshape(nseq * seq, d)
    xs = x_sample.reshape(nreq * t_new, d)
    dil_caches = (cache_dil_w128, cache_dil_w512, cache_dil_w2048)
    pos_p = np.arange(seq)
    pos_s = PAST_LEN + np.arange(t_new)
    rows_s = min(nreq * t_new, TM_SAMPLE)
    rope_p = {dh: _rope_tables(pos_p, dh) for dh in (HEAD_DIM, DK_R)}
    rope_s = {dh: _rope_tables(np.tile(pos_s, rows_s // t_new), dh) for dh in (HEAD_DIM, DK_R)}
    mix_widths = (N_DIL * HW, 2 * HW, 2 * HW, 2 * HW, HW + 2 * KVB)

    rope_t = _rope_tables_t(pos_p)
    n_even = (depth + 1) // 2
    dil_p = [None for _ in DIL_PAIRS]
    swa_p = None
    dil_s = [[] for _ in DIL_PAIRS]
    swa_s, ret_p = [], []
    ret_s = None
    for layer in range(depth):
        gn = norm_gains[layer]
        if layer % 2 == 0:
            e = layer // 2
            w_in = _slot_order(w_in_mix[e], 1, N_DIL * 3 * HW).astype(BF16)
            w_out = _slot_order(w_out_mix[e], 0, HW).astype(BF16)
            qa_p, *kv_p, zb_p = _norm_proj(_mix_proj_body, xp, gn[0], w_in, *rope_p[HEAD_DIM], mix_widths, TM_PROJ)
            qa_s, *kv_s, zb_s = _norm_proj(_mix_proj_body, xs, gn[0], w_in, *rope_s[HEAD_DIM], mix_widths, rows_s)
            kv_cols = lambda lo, hi: w_in_mix[e][:, lo:hi].T.astype(BF16)
            for g, (win, _) in enumerate(DIL_PAIRS):
                wt = kv_cols(g * 3 * HW + HW, (g + 1) * 3 * HW)
                dil_p[g] = _cache_rows(xp, gn[0], wt, *rope_t, dil_p[g], e, n_even, nseq, seq, min(win, seq),
                                       TM_CACHE_ROWS)
            wt = kv_cols(N_DIL * 3 * HW + HW, N_DIL * 3 * HW + HW + 2 * KVB)
            swa_p = _cache_rows(xp, gn[0], wt, *rope_t, swa_p, e, n_even, nseq, seq, min(SWA_WINDOW, seq),
                                TM_CACHE_ROWS)
            oa_p, ob_p = _prompt_attn(qa_p, kv_p, zb_p, attn_sinks[e], nseq, seq)
            xp = _out_proj([oa_p, ob_p], xp, gn[1], w_out, TM_OUT_PROJ)
            side = _sample_attn(qa_s, kv_s, zb_s, dil_caches, cache_swa, attn_sinks[e], e, nreq, t_new)
            for g in range(N_DIL):
                dil_s[g].append(kv_s[g].reshape(nreq, t_new, 2, N_HEADS, HEAD_DIM))
            swa_s.append(zb_s[:, HW:].reshape(nreq, t_new, 2, H_BKV, HEAD_DIM))
        else:
            o = layer // 2
            w_in = w_in_ret[o].astype(BF16)
            w_out = w_out_ret[o].astype(BF16)
            width = (w_in.shape[1],)
            zr_p, = _norm_proj(_ret_proj_body, xp, gn[0], w_in, *rope_p[DK_R], width, TM_PROJ, out_dtype=BF16)
            zr_s, = _norm_proj(_ret_proj_body, xs, gn[0], w_in, *rope_s[DK_R], width, rows_s)
            y_p, st_p = _ret_prompt(zr_p, nseq, seq)
            side = _ret_sample(zr_s, state_ret, ret_s, o, nreq, t_new, 1)
            xp = _out_proj([y_p], xp, gn[1], w_out, TM_OUT_PROJ)
            ret_p.append(st_p)
        w1 = w_ff1[layer].astype(BF16)
        w2 = w_ff2[layer].astype(BF16)
        xp, side_out = _mlp(xp, gn[2], gn[3], w1, w2, TM_MLP, TF_MLP, side)
        if layer % 2 == 1:
            ret_s = side_out[1]
        y_s = side_out[0].reshape(nreq * t_new, side_out[0].shape[-1])
        xs = _out_proj([y_s], xs, gn[1], w_out, TM_SAMPLE)
        xs = _mlp(xs, gn[2], gn[3], w1, w2, TM_SAMPLE, TF_MLP)
    def rows_out(buf, heads):
        n_l, n_s, _, wb = buf.shape
        return jnp.transpose(buf.reshape(n_l, n_s, 2, heads, HEAD_DIM, wb), (0, 1, 5, 2, 3, 4))

    return (xp.reshape(nseq, seq, d), xs.reshape(nreq, t_new, d),
            rows_out(dil_p[0], N_HEADS), jnp.stack(dil_s[0]),
            rows_out(dil_p[1], N_HEADS), jnp.stack(dil_s[1]),
            rows_out(dil_p[2], N_HEADS), jnp.stack(dil_s[2]),
            rows_out(swa_p, H_BKV), jnp.stack(swa_s),
            jnp.stack(ret_p), ret_s)
```

```python
import functools

import numpy as np
import jax
import jax.numpy as jnp
from jax import lax
from jax.experimental import pallas as pl
from jax.experimental.pallas import tpu as pltpu

F32 = jnp.float32
BF16 = jnp.bfloat16

HEAD_DIM = 64
ROPE_HALF = HEAD_DIM // 2
N_HEADS = 8
HW = N_HEADS * HEAD_DIM
DIL_PAIRS = ((128, 1), (512, 4), (2048, 16))
N_DIL = len(DIL_PAIRS)
H_BKV = 2
G_B = N_HEADS // H_BKV
KVB = H_BKV * HEAD_DIM
SWA_WINDOW = 128
H_R = 4
DK_R = 256
DV_R = 512
RET_CHUNK = 128
BLOCK = 128
PAST_LEN = 8192
ROPE_THETA = 10000.0
NORM_EPS = 1e-6
GN_EPS = 1e-5
NEG_INF = -1e30
LANES = 128
BLOCKS_IN_FLIGHT = 3
FIRST_BLOCKS_IN_FLIGHT = 4
N_MLP_IN = 5
TM_PROJ = 512
TM_OUT_PROJ = 512
TM_CACHE_ROWS = 512
TM_MLP, TF_MLP = 1024, 512
TM_SAMPLE = 512
RET_REQS_PER_STEP = 2
Q_PAD = 8
Q_SHIFT = Q_PAD.bit_length() - 1
HEAD_SHIFT = HEAD_DIM.bit_length() - 1
VMEM_LIMIT = 56 * 1024 * 1024


def _swa_slot(hq):
    return (hq % G_B) * H_BKV + hq // G_B


def _slot_order(w, axis, base):
    piece = lambda lo, hi: lax.slice_in_dim(w, lo, hi, axis=axis)
    q = piece(base, base + HW)
    shape = q.shape[:axis] + (H_BKV, G_B, HEAD_DIM) + q.shape[axis + 1:]
    q = jnp.swapaxes(q.reshape(shape), axis, axis + 1).reshape(q.shape)
    return jnp.concatenate([piece(0, base), q, piece(base + HW, w.shape[axis])], axis=axis)


def _params(sem):
    return pltpu.CompilerParams(dimension_semantics=sem, vmem_limit_bytes=VMEM_LIMIT)


def _rms(x, g):
    return x * lax.rsqrt(jnp.mean(x * x, axis=-1, keepdims=True) + NORM_EPS) * g


def _dot(a, b):
    return jnp.dot(a, b, preferred_element_type=F32)


def _dot_nt(a, b):
    return lax.dot_general(a, b, (((1,), (1,)), ((), ())), preferred_element_type=F32)


def _dot_tn(a, b):
    return lax.dot_general(a, b, (((0,), (0,)), ((), ())), preferred_element_type=F32)


def _rope_tables(pos, dh):
    half = dh // 2
    inv = ROPE_THETA ** (-np.arange(half, dtype=np.float64) * (2.0 / dh))
    ang = np.asarray(pos, np.float64)[:, None] * inv[None, :]
    c, s = np.cos(ang), np.sin(ang)
    if dh == HEAD_DIM:
        cos = np.tile(np.concatenate([c, c], axis=1), (1, LANES // dh))
        sin = np.tile(np.concatenate([-s, s], axis=1), (1, LANES // dh))
    else:
        cos, sin = c, s
    return jnp.asarray(cos, F32), jnp.asarray(sin, F32)


def _rope_tables_t(pos):
    inv = ROPE_THETA ** (-np.arange(ROPE_HALF, dtype=np.float64) * (2.0 / HEAD_DIM))
    ang = inv[:, None] * np.asarray(pos, np.float64)[None, :]
    return jnp.asarray(np.cos(ang), F32), jnp.asarray(np.sin(ang), F32)


def _cache_rows_body(*refs, rope_heads):
    x_ref, g_ref, wt_ref, cos_ref, sin_ref = refs[:5]
    out_ref = refs[-1]
    h = _rms(x_ref[...], g_ref[...]).astype(BF16)
    zt = _dot_nt(wt_ref[...], h)
    cos, sin = cos_ref[...], sin_ref[...]
    for hh in range(rope_heads):
        r0 = hh * HEAD_DIM
        k1, k2 = zt[r0:r0 + ROPE_HALF], zt[r0 + ROPE_HALF:r0 + HEAD_DIM]
        out_ref[r0:r0 + ROPE_HALF, :] = k1 * cos - k2 * sin
        out_ref[r0 + ROPE_HALF:r0 + HEAD_DIM, :] = k2 * cos + k1 * sin
    out_ref[rope_heads * HEAD_DIM:, :] = zt[rope_heads * HEAD_DIM:]


def _cache_rows(x, gain, wt, cos_t, sin_t, buf, e, n_layers, nseq, seq, wb, tm):
    d = x.shape[1]
    cols = wt.shape[0]
    tm = min(tm, wb)
    assert wb % tm == 0 and seq % tm == 0 and (seq - wb) % tm == 0
    tiles, first = seq // tm, (seq - wb) // tm
    in_specs = [
        pl.BlockSpec((tm, d), lambda n, j: (n * tiles + first + j, 0)),
        pl.BlockSpec((1, d), lambda n, j: (0, 0)),
        pl.BlockSpec((cols, d), lambda n, j: (0, 0)),
        pl.BlockSpec((ROPE_HALF, tm), lambda n, j: (0, first + j)),
        pl.BlockSpec((ROPE_HALF, tm), lambda n, j: (0, first + j)),
    ]
    args = [x, gain.reshape(1, d), wt, cos_t, sin_t]
    if buf is not None:
        in_specs.append(pl.BlockSpec(memory_space=pl.ANY))
        args.append(buf)
    return pl.pallas_call(
        functools.partial(_cache_rows_body, rope_heads=cols // (2 * HEAD_DIM)),
        grid=(nseq, wb // tm),
        in_specs=in_specs,
        out_specs=pl.BlockSpec((None, None, cols, tm), lambda n, j: (e, n, 0, j)),
        out_shape=jax.ShapeDtypeStruct((n_layers, nseq, cols, wb), F32),
        input_output_aliases={5: 0} if buf is not None else {},
        compiler_params=_params(("parallel", "parallel")),
    )(*args)


def _rope64(z, cos, sin):
    w = z.shape[1]
    reps = w // LANES
    lane = lax.broadcasted_iota(jnp.int32, z.shape, 1)
    first_half = (lane & (HEAD_DIM - 1)) < ROPE_HALF
    partner = jnp.where(first_half, pltpu.roll(z, w - ROPE_HALF, 1), pltpu.roll(z, ROPE_HALF, 1))
    if reps > 1:
        cos = jnp.concatenate([cos] * reps, axis=1)
        sin = jnp.concatenate([sin] * reps, axis=1)
    return z * cos + partner * sin


def _mix_proj_body(x_ref, g_ref, w_ref, cos_ref, sin_ref, qa_ref, kv0_ref, kv1_ref, kv2_ref, zb_ref):
    h = _rms(x_ref[...], g_ref[...]).astype(BF16)
    cos, sin = cos_ref[...], sin_ref[...]
    kv_refs = (kv0_ref, kv1_ref, kv2_ref)
    for g in range(N_DIL):
        base = g * 3 * HW
        qa_ref[:, g * HW:(g + 1) * HW] = _rope64(_dot(h, w_ref[:, base:base + HW]), cos, sin)
        kv_refs[g][:, :HW] = _rope64(_dot(h, w_ref[:, base + HW:base + 2 * HW]), cos, sin)
        kv_refs[g][:, HW:] = _dot(h, w_ref[:, base + 2 * HW:base + 3 * HW])
    base = N_DIL * 3 * HW
    zb_ref[:, :HW] = _rope64(_dot(h, w_ref[:, base:base + HW]), cos, sin)
    zb_ref[:, HW:HW + KVB] = _rope64(_dot(h, w_ref[:, base + HW:base + HW + KVB]), cos, sin)
    zb_ref[:, HW + KVB:] = _dot(h, w_ref[:, base + HW + KVB:base + HW + 2 * KVB])


def _ret_proj_body(x_ref, g_ref, w_ref, cos_ref, sin_ref, z_ref):
    h = _rms(x_ref[...], g_ref[...]).astype(BF16)
    cos, sin = cos_ref[...], sin_ref[...]
    half = DK_R // 2
    for part, scale in ((0, 1.0), (1, DK_R ** -0.5)):
        for hh in range(H_R):
            c0 = part * H_R * DK_R + hh * DK_R
            z = _dot(h, w_ref[:, c0:c0 + DK_R])
            z1, z2 = z[:, :half], z[:, half:]
            z_ref[:, c0:c0 + half] = ((z1 * cos - z2 * sin) * scale).astype(z_ref.dtype)
            z_ref[:, c0 + half:c0 + DK_R] = ((z2 * cos + z1 * sin) * scale).astype(z_ref.dtype)
    c0 = 2 * H_R * DK_R
    for j in range(2 * H_R):
        cols = slice(c0 + j * DV_R, c0 + (j + 1) * DV_R)
        z_ref[:, cols] = _dot(h, w_ref[:, cols]).astype(z_ref.dtype)


def _norm_proj(body, x, gain, w, cos, sin, out_widths, tm, out_dtype=F32):
    m, d = x.shape
    tm = min(tm, m)
    assert m % tm == 0 and cos.shape[0] % tm == 0
    period = cos.shape[0] // tm
    n = w.shape[1]
    outs = pl.pallas_call(
        body,
        grid=(m // tm,),
        in_specs=[
            pl.BlockSpec((tm, d), lambda i: (i, 0)),
            pl.BlockSpec((1, d), lambda i: (0, 0)),
            pl.BlockSpec((d, n), lambda i: (0, 0), pipeline_mode=pl.Buffered(1)),
            pl.BlockSpec((tm, LANES), lambda i: (i % period, 0)),
            pl.BlockSpec((tm, LANES), lambda i: (i % period, 0)),
        ],
        out_specs=[pl.BlockSpec((tm, wd), lambda i: (i, 0)) for wd in out_widths],
        out_shape=[jax.ShapeDtypeStruct((m, wd), out_dtype) for wd in out_widths],
        compiler_params=_params(("parallel",)),
    )(x, gain.reshape(1, d), w, cos, sin)
    return outs


def _attn_pairs(blocks, mask, sinks):
    lane = lax.broadcasted_iota(jnp.int32, (BLOCK, LANES), 1)
    first = lane < HEAD_DIM
    owns = (first, jnp.logical_not(first))
    both = [(b, half) for b in range(len(blocks)) for half in range(len(owns))]
    kbs = [k.astype(BF16) for _, k, _ in blocks]
    vbs = [jnp.concatenate([v.astype(BF16), jnp.ones(v.shape, BF16)], axis=1) for _, _, v in blocks]
    s = [jnp.where(mask, _dot_nt(jnp.where(owns[half], blocks[b][0], 0.0).astype(BF16), kbs[b]), NEG_INF)
         for b, half in both]
    m = [jnp.max(x, axis=-1, keepdims=True) for x in s]
    if sinks is not None:
        m = [jnp.maximum(x, sinks[half]) for x, (_, half) in zip(m, both)]
    p = [jnp.exp(x - y).astype(BF16) for x, y in zip(s, m)]
    pv_l = [_dot(x, vbs[b]) for x, (b, _) in zip(p, both)]
    out = []
    for b in range(len(blocks)):
        i0, i1 = len(owns) * b, len(owns) * b + 1
        l0, l1 = pv_l[i0][:, LANES:], pv_l[i1][:, LANES:]
        if sinks is not None:
            l0 = l0 + jnp.exp(sinks[0] - m[i0])
            l1 = l1 + jnp.exp(sinks[1] - m[i1])
        out.append((jnp.where(first, m[i0], m[i1]), jnp.where(first, l0, l1),
                    jnp.where(first, pv_l[i0][:, :LANES], pv_l[i1][:, :LANES])))
    return out


def _prompt_attn_body(sink_ref, q0_ref, k0_ref, v0_ref, q1_ref, k1_ref, v1_ref, q2_ref, k2_ref, v2_ref,
                      qb_ref, kb_ref, vb_ref, oa_ref, ob_ref, acc_ref, m_ref, l_ref, *, seq):
    pair = pl.program_id(1)
    scale = HEAD_DIM ** -0.5
    row = lax.broadcasted_iota(jnp.int32, (BLOCK, 2 * BLOCK), 0)
    col = lax.broadcasted_iota(jnp.int32, (BLOCK, 2 * BLOCK), 1)
    causal = (lax.broadcasted_iota(jnp.int32, (BLOCK, BLOCK), 1)
              <= lax.broadcasted_iota(jnp.int32, (BLOCK, BLOCK), 0))

    def window_mask(min_back):
        return jnp.where(col < BLOCK, col - row - min_back, row - (col - BLOCK)) >= 0

    def rows(ref, start, n, stride):
        return ref[pl.ds(start, n, stride=stride), :] if stride > 1 else ref[pl.ds(start, n), :]

    def group_pass(q_ref, k_ref, v_ref, dil, min_back, sinks, emit):
        span = dil * BLOCK
        nblk = seq // span

        def run(starts, back, keys, mask):
            blocks = [(rows(q_ref, s, BLOCK, dil) * scale, rows(k_ref, s - back, keys, dil),
                       rows(v_ref, s - back, keys, dil)) for s in starts]
            for s, mlp in zip(starts, _attn_pairs(blocks, mask, sinks)):
                emit(s, mlp)

        for r0 in range(0, dil, FIRST_BLOCKS_IN_FLIGHT):
            run(list(range(r0, min(r0 + FIRST_BLOCKS_IN_FLIGHT, dil))), 0, BLOCK, causal)
        if nblk == 1:
            return
        mask = window_mask(min_back)
        per_iter = BLOCKS_IN_FLIGHT if (nblk - 1) % BLOCKS_IN_FLIGHT == 0 else 1
        trips = (nblk - 1) // per_iter

        def block_start(r, lb):
            return lb * span + r if isinstance(lb, int) else pl.multiple_of(lb * span, span) + r

        for r in range(dil):
            def body(it, carry, r=r):
                run([block_start(r, 1 + it * per_iter + j) for j in range(per_iter)], span, 2 * BLOCK, mask)
                return carry

            if trips == 1:
                body(0, 0)
            else:
                lax.fori_loop(0, trips, body, 0)

    def merge_emit(dil, last):
        def emit(start, mlp):
            m, l, pv = mlp
            idx = pl.ds(start, BLOCK, stride=dil)
            m_old = m_ref[idx, :]
            m_new = jnp.maximum(m_old, m)
            a = jnp.exp(m_old - m_new)
            b = jnp.exp(m - m_new)
            acc = a * acc_ref[idx, :] + b * pv
            den = a * l_ref[idx, :] + b * l
            if last:
                oa_ref[idx, :] = acc / den
            else:
                m_ref[idx, :] = m_new
                acc_ref[idx, :] = acc
                l_ref[idx, :] = den
        return emit

    def first_emit(start, mlp):
        idx = pl.ds(start, BLOCK)
        m_ref[idx, :], l_ref[idx, :], acc_ref[idx, :] = mlp

    groups = ((q0_ref, k0_ref, v0_ref), (q1_ref, k1_ref, v1_ref), (q2_ref, k2_ref, v2_ref))
    for g, (win, dil) in enumerate(DIL_PAIRS):
        emit = first_emit if g == 0 else merge_emit(dil, last=g == N_DIL - 1)
        group_pass(*groups[g], dil, BLOCK - win // dil, None, emit)

    sinks = [sink_ref[0, pair + G_B * half] for half in range(H_BKV)]

    def swa_emit(start, mlp):
        m, l, pv = mlp
        ob_ref[pl.ds(start, BLOCK), :] = pv / l

    group_pass(qb_ref, kb_ref, vb_ref, 1, BLOCK + 1 - SWA_WINDOW, sinks, swa_emit)


def _prompt_attn(qa, kvs, zb, sinks, nseq, seq):
    assert seq % (BLOCK * DIL_PAIRS[-1][1]) == 0 and KVB == LANES
    pairs = HW // LANES
    view = lambda a: a.reshape(nseq, seq, a.shape[1])
    col = lambda j: pl.BlockSpec((None, seq, LANES), lambda n, p: (n, 0, j(p)))
    in_specs = [pl.BlockSpec(memory_space=pltpu.SMEM)]
    args = [sinks.reshape(1, N_HEADS)]
    for g in range(N_DIL):
        in_specs += [col(lambda p, g=g: g * pairs + p), col(lambda p: p), col(lambda p: pairs + p)]
        args += [view(qa), view(kvs[g]), view(kvs[g])]
    in_specs += [col(lambda p: p), col(lambda p: pairs), col(lambda p: pairs + 1)]
    args += [view(zb)] * 3
    out_spec = pl.BlockSpec((None, seq, LANES), lambda n, p: (n, 0, p))
    oa, ob = pl.pallas_call(
        functools.partial(_prompt_attn_body, seq=seq),
        grid=(nseq, pairs),
        in_specs=in_specs,
        out_specs=[out_spec, out_spec],
        out_shape=[jax.ShapeDtypeStruct((nseq, seq, HW), F32)] * 2,
        scratch_shapes=[pltpu.VMEM((seq, LANES), F32)] * 3,
        compiler_params=_params(("parallel", "parallel")),
    )(*args)
    return oa.reshape(nseq * seq, HW), ob.reshape(nseq * seq, HW)


def _sample_attn_body(qa_ref, kn0_ref, kn1_ref, kn2_ref, zb_ref, c0_ref, c1_ref, c2_ref, cs_ref, sink_ref,
                      y_ref, pvn_ref, *, t_new, first):
    del first
    scale = HEAD_DIM ** -0.5
    r_i = lax.broadcasted_iota(jnp.int32, (N_HEADS * Q_PAD, HW), 0)
    l_i = lax.broadcasted_iota(jnp.int32, (N_HEADS * Q_PAD, HW), 1)
    own_head = (r_i >> Q_SHIFT) == (l_i >> HEAD_SHIFT)
    t_q = lax.broadcasted_iota(jnp.int32, (Q_PAD, 1), 0)
    new_key = lax.broadcasted_iota(jnp.int32, (Q_PAD, BLOCK), 1)

    def pad_rows(x, n):
        return jnp.concatenate([x, jnp.zeros((n - x.shape[0], x.shape[1]), x.dtype)], axis=0)

    def head_cols(ref, base, h):
        return ref[:, base + h * HEAD_DIM:base + (h + 1) * HEAD_DIM]

    kn_refs = (kn0_ref, kn1_ref, kn2_ref)
    c_refs = (c0_ref, c1_ref, c2_ref)
    stats = [[] for _ in range(N_HEADS)]
    for g, (win, dil) in enumerate(DIL_PAIRS):
        c_ref = c_refs[g]
        wb = c_ref.shape[-1]
        q_all = pad_rows(qa_ref[:, g * HW:(g + 1) * HW] * scale, Q_PAD)
        q_exp = jnp.where(own_head, jnp.concatenate([q_all] * N_HEADS, axis=0), 0.0).astype(BF16)
        s_new_all = _dot_nt(q_exp, pad_rows(kn_refs[g][:, :HW].astype(BF16), BLOCK))
        back = t_q - new_key
        mask_new = jnp.logical_and(jnp.logical_and(back >= 0, new_key < t_new), (back & (dil - 1)) == 0)
        ahead = lax.broadcasted_iota(jnp.int32, (Q_PAD, wb), 1) - t_q
        mask_cache = jnp.logical_and(ahead >= 0, (ahead & (dil - 1)) == 0)
        p_news = []
        partial = []
        for h in range(N_HEADS):
            qh = pad_rows(head_cols(qa_ref, g * HW, h) * scale, Q_PAD).astype(BF16)
            s_c = jnp.where(mask_cache, _dot(qh, c_ref[0, h].astype(BF16)), NEG_INF)
            s_n = jnp.where(mask_new, s_new_all[h * Q_PAD:(h + 1) * Q_PAD], NEG_INF)
            m = jnp.maximum(jnp.max(s_c, axis=-1, keepdims=True), jnp.max(s_n, axis=-1, keepdims=True))
            p_c = jnp.exp(s_c - m)
            p_n = jnp.exp(s_n - m)
            l = jnp.sum(p_c, axis=-1, keepdims=True) + jnp.sum(p_n, axis=-1, keepdims=True)
            partial.append((m, l, _dot_nt(p_c.astype(BF16), c_ref[1, h].astype(BF16))))
            p_news.append(p_n.astype(BF16))
        pvn_ref[...] = _dot(jnp.concatenate(p_news, axis=0), pad_rows(kn_refs[g][:, HW:].astype(BF16), BLOCK))
        for h, (m, l, pv) in enumerate(partial):
            pv = pv + pvn_ref[h * Q_PAD:(h + 1) * Q_PAD, h * HEAD_DIM:(h + 1) * HEAD_DIM]
            stats[h].append((m, l, pv))
    for h in range(N_HEADS):
        m_all = functools.reduce(jnp.maximum, [s[0] for s in stats[h]])
        acc = jnp.zeros((Q_PAD, HEAD_DIM), F32)
        den = jnp.zeros((Q_PAD, 1), F32)
        for m, l, pv in stats[h]:
            a = jnp.exp(m - m_all)
            acc = acc + a * pv
            den = den + a * l
        y_ref[:, h * HEAD_DIM:(h + 1) * HEAD_DIM] = (acc / den)[0:t_new]

    t_g = lax.broadcasted_iota(jnp.int32, (G_B * Q_PAD, 1), 0) & (Q_PAD - 1)
    key_g = lax.broadcasted_iota(jnp.int32, (G_B * Q_PAD, BLOCK), 1)
    mask_cache = key_g >= t_g + 1 + (BLOCK - SWA_WINDOW)
    mask_new = jnp.logical_and(t_g - key_g >= 0, key_g < t_new)
    for hk in range(H_BKV):
        qs = jnp.concatenate([pad_rows(head_cols(zb_ref, 0, _swa_slot(hk * G_B + gq)) * scale, Q_PAD)
                              for gq in range(G_B)], axis=0).astype(BF16)
        k_new = pad_rows(head_cols(zb_ref, HW, hk).astype(BF16), BLOCK)
        v_new = pad_rows(head_cols(zb_ref, HW + KVB, hk).astype(BF16), BLOCK)
        s_c = jnp.where(mask_cache, _dot(qs, cs_ref[0, hk].astype(BF16)), NEG_INF)
        s_n = jnp.where(mask_new, _dot_nt(qs, k_new), NEG_INF)
        sink = sink_ref[hk * G_B * Q_PAD:(hk + 1) * G_B * Q_PAD, 0:1]
        m = jnp.maximum(jnp.maximum(jnp.max(s_c, axis=-1, keepdims=True), jnp.max(s_n, axis=-1, keepdims=True)), sink)
        p_c = jnp.exp(s_c - m)
        p_n = jnp.exp(s_n - m)
        l = jnp.sum(p_c, axis=-1, keepdims=True) + jnp.sum(p_n, axis=-1, keepdims=True) + jnp.exp(sink - m)
        o = (_dot_nt(p_c.astype(BF16), cs_ref[1, hk].astype(BF16)) + _dot(p_n.astype(BF16), v_new)) / l
        for gq in range(G_B):
            slot = _swa_slot(hk * G_B + gq)
            y_ref[:, HW + slot * HEAD_DIM:HW + (slot + 1) * HEAD_DIM] = o[gq * Q_PAD:gq * Q_PAD + t_new]


def _run_side(side):
    d = side(lambda b: b)

    def body(*refs):
        d["body"](*refs, first=pl.program_id(0) == 0)

    return pl.pallas_call(
        body,
        grid=(d["steps"],),
        in_specs=d["in_specs"],
        out_specs=d["out_specs"],
        out_shape=d["out_shape"],
        input_output_aliases=d["aliases"],
        scratch_shapes=d["scratch"],
        compiler_params=_params(("arbitrary",)),
    )(*d["args"])


def _sample_attn(qa, kns, zb, caches, cache_swa, sinks, e, nreq, t_new):
    assert t_new <= Q_PAD
    qv = qa.reshape(nreq, t_new, N_DIL * HW)
    knv = [k.reshape(nreq, t_new, 2 * HW) for k in kns]
    zbv = zb.reshape(nreq, t_new, HW + 2 * KVB)
    to_native = lambda c: jnp.transpose(c, (0, 1, 3, 4, 5, 2))
    for (win, dil), c in zip(DIL_PAIRS, caches):
        assert c.shape[2] == win and win // dil == BLOCK
    assert cache_swa.shape[2] == BLOCK
    cv = [to_native(c) for c in caches] + [to_native(cache_swa)]
    sink_rows = jnp.broadcast_to(jnp.repeat(sinks, Q_PAD)[:, None], (N_HEADS * Q_PAD, LANES))

    def side(step):
        tok = lambda wd: pl.BlockSpec((None, t_new, wd), lambda *g: (step(*g), 0, 0))
        cache = lambda c: pl.BlockSpec((None, None) + c.shape[2:], lambda *g: (e, step(*g), 0, 0, 0, 0))
        return dict(
            body=functools.partial(_sample_attn_body, t_new=t_new),
            steps=nreq,
            in_specs=[tok(N_DIL * HW), tok(2 * HW), tok(2 * HW), tok(2 * HW), tok(HW + 2 * KVB)]
            + [cache(c) for c in cv] + [pl.BlockSpec((N_HEADS * Q_PAD, LANES), lambda *g: (0, 0))],
            args=[qv, *knv, zbv, *cv, sink_rows],
            out_specs=[tok(2 * HW)],
            out_shape=[jax.ShapeDtypeStruct((nreq, t_new, 2 * HW), F32)],
            aliases={},
            scratch=[pltpu.VMEM((N_HEADS * Q_PAD, HW), F32)],
        )

    return side


def _out_proj_body(*refs):
    *y_refs, x_ref, g_ref, w_ref, out_ref = refs
    mo = None
    r0 = 0
    for y_ref in y_refs:
        part = _dot(y_ref[...].astype(BF16), w_ref[r0:r0 + y_ref.shape[1], :])
        mo = part if mo is None else mo + part
        r0 += y_ref.shape[1]
    out_ref[...] = x_ref[...] + _rms(mo, g_ref[...])


def _out_proj(ys, x, gain, w, tm):
    m, d = x.shape
    tm = min(tm, m)
    assert m % tm == 0 and sum(y.shape[1] for y in ys) == w.shape[0]
    row = lambda wd: pl.BlockSpec((tm, wd), lambda i: (i, 0))
    return pl.pallas_call(
        _out_proj_body,
        grid=(m // tm,),
        in_specs=[row(y.shape[1]) for y in ys] + [
            row(d),
            pl.BlockSpec((1, d), lambda i: (0, 0)),
            pl.BlockSpec(w.shape, lambda i: (0, 0), pipeline_mode=pl.Buffered(1)),
        ],
        out_specs=row(d),
        out_shape=jax.ShapeDtypeStruct((m, d), F32),
        compiler_params=_params(("parallel",)),
    )(*ys, x, gain.reshape(1, d), w)


def _mlp_body(x_ref, g_pre_ref, g_post_ref, w1_ref, w2_ref, out_ref, h_ref, acc_ref):
    j = pl.program_id(1)

    @pl.when(j == 0)
    def _():
        h_ref[...] = _rms(x_ref[...], g_pre_ref[...]).astype(BF16)
        acc_ref[...] = jnp.zeros_like(acc_ref)

    a = jnp.maximum(_dot(h_ref[...], w1_ref[...]), 0.0)
    acc_ref[...] += _dot((a * a).astype(BF16), w2_ref[...])

    @pl.when(j == pl.num_programs(1) - 1)
    def _():
        out_ref[...] = x_ref[...] + _rms(acc_ref[...], g_post_ref[...])


def _mlp_and_side_body(*refs, n_in, n_out, side_body):
    mlp_in, side_in = refs[:N_MLP_IN], refs[N_MLP_IN:N_MLP_IN + n_in]
    out_ref = refs[N_MLP_IN + n_in]
    side_out = refs[N_MLP_IN + n_in + 1:N_MLP_IN + n_in + 1 + n_out]
    h_ref, acc_ref, *side_scratch = refs[N_MLP_IN + n_in + 1 + n_out:]
    _mlp_body(*mlp_in, out_ref, h_ref, acc_ref)
    first = jnp.logical_and(pl.program_id(0) == 0, pl.program_id(1) == 0)
    side_body(*side_in, *side_out, *side_scratch, first=first)


def _mlp(x, g_pre, g_post, w1, w2, tm, tf, side=None):
    m, d = x.shape
    ff = w1.shape[1]
    tm = min(tm, m)
    assert m % tm == 0 and ff % tf == 0
    grid = (m // tm, ff // tf)
    in_specs = [
        pl.BlockSpec((tm, d), lambda i, j: (i, 0)),
        pl.BlockSpec((1, d), lambda i, j: (0, 0)),
        pl.BlockSpec((1, d), lambda i, j: (0, 0)),
        pl.BlockSpec((d, tf), lambda i, j: (0, j)),
        pl.BlockSpec((tf, d), lambda i, j: (j, 0)),
    ]
    args = [x, g_pre.reshape(1, d), g_post.reshape(1, d), w1, w2]
    assert len(args) == N_MLP_IN
    out_spec = pl.BlockSpec((tm, d), lambda i, j: (i, 0))
    out_shape = jax.ShapeDtypeStruct((m, d), F32)
    scratch = [pltpu.VMEM((tm, d), BF16), pltpu.VMEM((tm, d), F32)]
    if side is None:
        return pl.pallas_call(
            _mlp_body, grid=grid, in_specs=in_specs, out_specs=out_spec, out_shape=out_shape,
            scratch_shapes=scratch, compiler_params=_params(("parallel", "arbitrary")),
        )(*args)
    s = side(lambda i, j: i * grid[1] + j)
    if s["steps"] != grid[0] * grid[1]:
        return _mlp(x, g_pre, g_post, w1, w2, tm, tf), _run_side(side)
    in_specs[0] = pl.BlockSpec((tm, d), lambda i, j: (i, 0), pipeline_mode=pl.Buffered(1))
    outs = pl.pallas_call(
        functools.partial(_mlp_and_side_body, n_in=len(s["in_specs"]), n_out=len(s["out_specs"]),
                          side_body=s["body"]),
        grid=grid,
        in_specs=in_specs + s["in_specs"],
        out_specs=[out_spec] + s["out_specs"],
        out_shape=[out_shape] + s["out_shape"],
        input_output_aliases={N_MLP_IN + i: 1 + o for i, o in s["aliases"].items()},
        scratch_shapes=scratch + s["scratch"],
        compiler_params=_params(("arbitrary", "arbitrary")),
    )(*args, *s["args"])
    return outs[0], outs[1:]


def _log_gammas():
    return jnp.asarray(np.log(1.0 - 2.0 ** (-5.0 - np.arange(H_R, dtype=np.float64))), F32)


def _group_norm_gate(o, gate):
    mu = jnp.mean(o, axis=-1, keepdims=True)
    var = jnp.mean(jnp.square(o - mu), axis=-1, keepdims=True)
    o = (o - mu) * lax.rsqrt(var + GN_EPS)
    return gate / (1.0 + jnp.exp(-gate)) * o


def _ret_prompt_body(lg_ref, q_ref, k_ref, v_ref, gate_ref, y_ref, st_ref, s_ref):
    c = pl.program_id(1)

    @pl.when(c == 0)
    def _():
        s_ref[...] = jnp.zeros_like(s_ref)

    chunk = q_ref.shape[0]
    row = lax.broadcasted_iota(jnp.int32, (chunk, chunk), 0)
    col = lax.broadcasted_iota(jnp.int32, (chunk, chunk), 1)
    diff = (row - col).astype(F32)
    ii = lax.broadcasted_iota(jnp.int32, (chunk, 1), 0).astype(F32)
    for h in range(H_R):
        lg = lg_ref[h]
        decay = jnp.where(diff >= 0, jnp.exp(jnp.maximum(diff, 0.0) * lg), 0.0)
        q = q_ref[:, h * DK_R:(h + 1) * DK_R]
        k = k_ref[:, h * DK_R:(h + 1) * DK_R]
        v = v_ref[:, h * DV_R:(h + 1) * DV_R].astype(BF16)
        a = _dot_nt(q.astype(BF16), k.astype(BF16)) * decay
        s_prev = s_ref[h]
        o = _dot(a.astype(BF16), v) + _dot((q * jnp.exp((ii + 1.0) * lg)).astype(BF16), s_prev.astype(BF16))
        kd = (k * jnp.exp((chunk - 1.0 - ii) * lg)).astype(BF16)
        s_ref[h] = jnp.exp(jnp.full((1, 1), chunk, F32) * lg) * s_prev + _dot_tn(kd, v)
        gate = gate_ref[:, h * DV_R:(h + 1) * DV_R].astype(F32)
        y_ref[:, h * DV_R:(h + 1) * DV_R] = _group_norm_gate(o, gate).astype(y_ref.dtype)

    @pl.when(c == pl.num_programs(1) - 1)
    def _():
        st_ref[...] = s_ref[...]


def _ret_prompt(zr, nseq, seq):
    nc = seq // RET_CHUNK
    zv = zr.reshape(nseq, seq, zr.shape[1])
    qk_w, v_w = H_R * DK_R, H_R * DV_R
    assert v_w == 2 * qk_w
    y, st = pl.pallas_call(
        _ret_prompt_body,
        grid=(nseq, nc),
        in_specs=[
            pl.BlockSpec(memory_space=pltpu.SMEM),
            pl.BlockSpec((None, RET_CHUNK, qk_w), lambda n, c: (n, c, 0)),
            pl.BlockSpec((None, RET_CHUNK, qk_w), lambda n, c: (n, c, 1)),
            pl.BlockSpec((None, RET_CHUNK, v_w), lambda n, c: (n, c, 1)),
            pl.BlockSpec((None, RET_CHUNK, v_w), lambda n, c: (n, c, 2)),
        ],
        out_specs=[
            pl.BlockSpec((None, RET_CHUNK, v_w), lambda n, c: (n, c, 0)),
            pl.BlockSpec((None, H_R, DK_R, DV_R), lambda n, c: (n, 0, 0, 0)),
        ],
        out_shape=[
            jax.ShapeDtypeStruct((nseq, seq, v_w), BF16),
            jax.ShapeDtypeStruct((nseq, H_R, DK_R, DV_R), F32),
        ],
        scratch_shapes=[pltpu.VMEM((H_R, DK_R, DV_R), F32)],
        compiler_params=_params(("parallel", "arbitrary")),
    )(_log_gammas(), zv, zv, zv, zv)
    return y.reshape(nseq * seq, v_w), st


def _ret_sample_body(*refs, t_new, has_prev, first):
    lg_ref, z_ref, s_in_ref = refs[:3]
    y_ref, s_out_ref, kpad_ref, vpad_ref = refs[4:] if has_prev else refs[3:]

    @pl.when(first)
    def _():
        kpad_ref[...] = jnp.zeros_like(kpad_ref)
        vpad_ref[...] = jnp.zeros_like(vpad_ref)

    rows = Q_PAD
    row = lax.broadcasted_iota(jnp.int32, (rows, BLOCK), 0)
    col = lax.broadcasted_iota(jnp.int32, (rows, BLOCK), 1)
    diff = (row - col).astype(F32)
    ii = lax.broadcasted_iota(jnp.int32, (rows, 1), 0).astype(F32)
    ik = lax.broadcasted_iota(jnp.int32, (t_new, 1), 0).astype(F32)
    zq = jnp.zeros((rows - t_new, DK_R), F32)
    for b, h in [(b, h) for b in range(z_ref.shape[0]) for h in range(H_R)]:
        lg = lg_ref[h]
        q = jnp.concatenate([z_ref[b, :, h * DK_R:(h + 1) * DK_R], zq], axis=0)
        k = z_ref[b, :, (H_R + h) * DK_R:(H_R + h + 1) * DK_R]
        v = z_ref[b, :, 2 * H_R * DK_R + h * DV_R:2 * H_R * DK_R + (h + 1) * DV_R]
        gate = z_ref[b, :, 2 * H_R * DK_R + (H_R + h) * DV_R:2 * H_R * DK_R + (H_R + h + 1) * DV_R]
        kpad_ref[0:t_new, :] = k
        vpad_ref[0:t_new, :] = v
        decay = jnp.where(diff >= 0, jnp.exp(jnp.maximum(diff, 0.0) * lg), 0.0)
        a = _dot_nt(q.astype(BF16), kpad_ref[...].astype(BF16)) * decay
        s_prev = s_in_ref[b, h]
        vb = vpad_ref[...].astype(BF16)
        o = _dot(a.astype(BF16), vb) + _dot((q * jnp.exp((ii + 1.0) * lg)).astype(BF16), s_prev.astype(BF16))
        kpad_ref[0:t_new, :] = k * jnp.exp((t_new - 1.0 - ik) * lg)
        upd = _dot_tn(kpad_ref[...].astype(BF16), vb)
        s_out_ref[b, h] = jnp.exp(jnp.full((1, 1), t_new, F32) * lg) * s_prev + upd
        y_ref[b, :, h * DV_R:(h + 1) * DV_R] = _group_norm_gate(o[0:t_new], gate)


def _ret_sample(zr, state, new_state, o, nreq, t_new, nb):
    zv = zr.reshape(nreq, t_new, zr.shape[1])
    has_prev = new_state is not None
    assert nreq % nb == 0

    def side(step):
        state_blk = pl.BlockSpec((None, nb, H_R, DK_R, DV_R), lambda *g: (o, step(*g), 0, 0, 0))
        in_specs = [
            pl.BlockSpec(memory_space=pltpu.SMEM),
            pl.BlockSpec((nb, t_new, zr.shape[1]), lambda *g: (step(*g), 0, 0)),
            state_blk,
        ]
        args = [_log_gammas(), zv, state]
        if has_prev:
            in_specs.append(pl.BlockSpec(memory_space=pl.ANY))
            args.append(new_state)
        return dict(
            body=functools.partial(_ret_sample_body, t_new=t_new, has_prev=has_prev),
            steps=nreq // nb,
            in_specs=in_specs,
            args=args,
            out_specs=[pl.BlockSpec((nb, t_new, H_R * DV_R), lambda *g: (step(*g), 0, 0)), state_blk],
            out_shape=[jax.ShapeDtypeStruct((nreq, t_new, H_R * DV_R), F32), jax.ShapeDtypeStruct(state.shape, F32)],
            aliases={3: 1} if has_prev else {},
            scratch=[pltpu.VMEM((BLOCK, DK_R), F32), pltpu.VMEM((BLOCK, DV_R), F32)],
        )

    return side


def kernel(x_prompt, x_sample, cache_dil_w128, cache_dil_w512, cache_dil_w2048, cache_swa, state_ret,
           norm_gains, w_in_mix, w_out_mix, attn_sinks, w_in_ret, w_out_ret, w_ff1, w_ff2):
    nseq, seq, d = x_prompt.shape
    nreq, t_new, _ = x_sample.shape
    depth = norm_gains.shape[0]
    assert seq % (BLOCK * DIL_PAIRS[-1][1]) == 0 and t_new <= DIL_PAIRS[1][1]
    xp = x_prompt.reshape(nseq * seq, d)
    xs = x_sample.reshape(nreq * t_new, d)
    dil_caches = (cache_dil_w128, cache_dil_w512, cache_dil_w2048)
    pos_p = np.arange(seq)
    pos_s = PAST_LEN + np.arange(t_new)
    rows_s = min(nreq * t_new, TM_SAMPLE)
    rope_p = {dh: _rope_tables(pos_p, dh) for dh in (HEAD_DIM, DK_R)}
    rope_s = {dh: _rope_tables(np.tile(pos_s, rows_s // t_new), dh) for dh in (HEAD_DIM, DK_R)}
    mix_widths = (N_DIL * HW, 2 * HW, 2 * HW, 2 * HW, HW + 2 * KVB)

    rope_t = _rope_tables_t(pos_p)
    n_even = (depth + 1) // 2
    dil_p = [None for _ in DIL_PAIRS]
    swa_p = None
    dil_s = [[] for _ in DIL_PAIRS]
    swa_s, ret_p = [], []
    ret_s = None
    for layer in range(depth):
        gn = norm_gains[layer]
        if layer % 2 == 0:
            e = layer // 2
            w_in = _slot_order(w_in_mix[e], 1, N_DIL * 3 * HW).astype(BF16)
            w_out = _slot_order(w_out_mix[e], 0, HW).astype(BF16)
            qa_p, *kv_p, zb_p = _norm_proj(_mix_proj_body, xp, gn[0], w_in, *rope_p[HEAD_DIM], mix_widths, TM_PROJ)
            qa_s, *kv_s, zb_s = _norm_proj(_mix_proj_body, xs, gn[0], w_in, *rope_s[HEAD_DIM], mix_widths, rows_s)
            kv_cols = lambda lo, hi: w_in_mix[e][:, lo:hi].T.astype(BF16)
            for g, (win, _) in enumerate(DIL_PAIRS):
                wt = kv_cols(g * 3 * HW + HW, (g + 1) * 3 * HW)
                dil_p[g] = _cache_rows(xp, gn[0], wt, *rope_t, dil_p[g], e, n_even, nseq, seq, min(win, seq),
                                       TM_CACHE_ROWS)
            wt = kv_cols(N_DIL * 3 * HW + HW, N_DIL * 3 * HW + HW + 2 * KVB)
            swa_p = _cache_rows(xp, gn[0], wt, *rope_t, swa_p, e, n_even, nseq, seq, min(SWA_WINDOW, seq),
                                TM_CACHE_ROWS)
            oa_p, ob_p = _prompt_attn(qa_p, kv_p, zb_p, attn_sinks[e], nseq, seq)
            xp = _out_proj([oa_p, ob_p], xp, gn[1], w_out, TM_OUT_PROJ)
            side = _sample_attn(qa_s, kv_s, zb_s, dil_caches, cache_swa, attn_sinks[e], e, nreq, t_new)
            for g in range(N_DIL):
                dil_s[g].append(kv_s[g].reshape(nreq, t_new, 2, N_HEADS, HEAD_DIM))
            swa_s.append(zb_s[:, HW:].reshape(nreq, t_new, 2, H_BKV, HEAD_DIM))
        else:
            o = layer // 2
            w_in = w_in_ret[o].astype(BF16)
            w_out = w_out_ret[o].astype(BF16)
            width = (w_in.shape[1],)
            zr_p, = _norm_proj(_ret_proj_body, xp, gn[0], w_in, *rope_p[DK_R], width, TM_PROJ, out_dtype=BF16)
            zr_s, = _norm_proj(_ret_proj_body, xs, gn[0], w_in, *rope_s[DK_R], width, rows_s)
            y_p, st_p = _ret_prompt(zr_p, nseq, seq)
            side = _ret_sample(zr_s, state_ret, ret_s, o, nreq, t_new, RET_REQS_PER_STEP)
            xp = _out_proj([y_p], xp, gn[1], w_out, TM_OUT_PROJ)
            ret_p.append(st_p)
        w1 = w_ff1[layer].astype(BF16)
        w2 = w_ff2[layer].astype(BF16)
        xp, side_out = _mlp(xp, gn[2], gn[3], w1, w2, TM_MLP, TF_MLP * (RET_REQS_PER_STEP if layer % 2 else 1), side)
        if layer % 2 == 1:
            ret_s = side_out[1]
        y_s = side_out[0].reshape(nreq * t_new, side_out[0].shape[-1])
        xs = _out_proj([y_s], xs, gn[1], w_out, TM_SAMPLE)
        xs = _mlp(xs, gn[2], gn[3], w1, w2, TM_SAMPLE, TF_MLP)
    def rows_out(buf, heads):
        n_l, n_s, _, wb = buf.shape
        return jnp.transpose(buf.reshape(n_l, n_s, 2, heads, HEAD_DIM, wb), (0, 1, 5, 2, 3, 4))

    return (xp.reshape(nseq, seq, d), xs.reshape(nreq, t_new, d),
            rows_out(dil_p[0], N_HEADS), jnp.stack(dil_s[0]),
            rows_out(dil_p[1], N_HEADS), jnp.stack(dil_s[1]),
            rows_out(dil_p[2], N_HEADS), jnp.stack(dil_s[2]),
            rows_out(swa_p, H_BKV), jnp.stack(swa_s),
            jnp.stack(ret_p), ret_s)
```
